```python
import math
import jax, jax.numpy as jnp
from jax import lax
import numpy as np

D_MODEL = 1024
BATCH = 1
SEQ = 16384
DEPTH = 4
DEC_BATCH = 8
DEC_SEQ = 4096
PAST_LEN = 128

GRID_W = 64
HEAD_DIM = 64
ATTN_HEADS = 8
ATTN_KV_HEADS = 2
RET_HEADS = 4
RET_KEY_DIM = HEAD_DIM
RET_VALUE_DIM = 2 * RET_KEY_DIM
RET_CHUNK = 128
Q_BLOCK = 128
D_FF = 2816
ROPE_THETA = 10000.0
ROPE_AXIS_PAIRS = HEAD_DIM // 4
NORM_EPS = 1e-6
ATTN_Q_W = ATTN_HEADS * HEAD_DIM
ATTN_KV_W = ATTN_KV_HEADS * HEAD_DIM
RET_QK_W = RET_HEADS * RET_KEY_DIM
RET_V_W = RET_HEADS * RET_VALUE_DIM
IN_PROJ_W = ATTN_Q_W + 2 * ATTN_KV_W + 2 * RET_QK_W + 2 * RET_V_W

kernel_name = 'hybrid_gqa_retention_macaron_encoder'


def rmsnorm(x, g):
    x32 = x.astype(jnp.float32)
    y = x32 * lax.rsqrt(jnp.mean(x32 * x32, axis=-1, keepdims=True) + NORM_EPS)
    return (y * g.astype(jnp.float32)).astype(x.dtype)


def swiglu_ffn(x, g, w13, w2):
    h = rmsnorm(x, g) @ w13
    a, b = jnp.split(h, 2, axis=-1)
    return (jax.nn.silu(a) * b) @ w2


def axial_rope_tables(T):
    rows = T // GRID_W
    row = jnp.repeat(jnp.arange(rows, dtype=jnp.float32), GRID_W)
    col = jnp.tile(jnp.arange(GRID_W, dtype=jnp.float32), rows)
    freqs = ROPE_THETA ** (-jnp.arange(ROPE_AXIS_PAIRS, dtype=jnp.float32) / ROPE_AXIS_PAIRS)
    ang = jnp.concatenate([row[:, None] * freqs, col[:, None] * freqs], axis=-1)
    return jnp.cos(ang), jnp.sin(ang)


def apply_rope(x, cos, sin):
    x32 = x.astype(jnp.float32)
    x1, x2 = jnp.split(x32, 2, axis=-1)
    c = cos[None, :, None, :]
    s = sin[None, :, None, :]
    return jnp.concatenate([x1 * c - x2 * s, x1 * s + x2 * c], axis=-1).astype(x.dtype)


def gqa_block_attention(q, k, v):
    B, T, H, hd = q.shape
    G = H // ATTN_KV_HEADS
    nb = T // Q_BLOCK
    qb = q.reshape(B, nb, Q_BLOCK, ATTN_KV_HEADS, G, hd).transpose(1, 0, 3, 4, 2, 5)
    scale = hd ** -0.5

    def block(qi):
        s = jnp.einsum('bkgqd,btkd->bkgqt', qi, k).astype(jnp.float32) * scale
        p = jax.nn.softmax(s, axis=-1).astype(v.dtype)
        return jnp.einsum('bkgqt,btkd->bkgqd', p, v)

    o = lax.map(block, qb)
    return o.transpose(1, 0, 4, 2, 3, 5).reshape(B, T, H * hd)


def retention_one_direction(q, k, v, decay_logit, include_diag):
    B, T, H, dk = q.shape
    dv = v.shape[-1]
    C = RET_CHUNK
    N = T // C
    log_gamma = jnp.log1p(-jnp.exp(decay_logit.astype(jnp.float32)))
    idx = jnp.arange(C, dtype=jnp.float32)
    diff = idx[:, None] - idx[None, :]
    mask = diff >= 0 if include_diag else diff > 0
    dmat = jnp.where(mask[None], jnp.exp(jnp.where(mask, diff, 0.0)[None] * log_gamma[:, None, None]), 0.0)
    q_decay = jnp.exp((idx[None, :] + 1.0) * log_gamma[:, None])[..., None]
    k_decay = jnp.exp((C - 1.0 - idx[None, :]) * log_gamma[:, None])[..., None]
    chunk_decay = jnp.exp(C * log_gamma)[:, None, None]

    def to_chunks(a):
        return a.reshape(B, N, C, H, a.shape[-1]).transpose(1, 0, 3, 2, 4)

    def step(S, inp):
        qi, ki, vi = inp
        inner = jnp.einsum('bhqd,bhkd->bhqk', qi, ki) * dmat
        o = jnp.einsum('bhqk,bhkv->bhqv', inner, vi) + jnp.einsum('bhqd,bhdv->bhqv', qi * q_decay, S)
        S = S * chunk_decay + jnp.einsum('bhkd,bhkv->bhdv', ki * k_decay, vi)
        return S, o

    S0 = jnp.zeros((B, H, dk, dv), jnp.float32)
    _, o = lax.scan(step, S0, (to_chunks(q), to_chunks(k), to_chunks(v)))
    return o.transpose(1, 0, 3, 2, 4).reshape(B, T, H, dv)


def bidirectional_retention(q, k, v, decay_fwd, decay_bwd):
    q32, k32, v32 = q.astype(jnp.float32), k.astype(jnp.float32), v.astype(jnp.float32)
    fwd = retention_one_direction(q32, k32, v32, decay_fwd, True)
    bwd = jnp.flip(retention_one_direction(jnp.flip(q32, 1), jnp.flip(k32, 1), jnp.flip(v32, 1), decay_bwd, False), 1)
    return fwd + bwd


def head_group_norm(y, g):
    mu = jnp.mean(y, axis=-1, keepdims=True)
    var = jnp.mean(jnp.square(y - mu), axis=-1, keepdims=True)
    yn = (y - mu) * lax.rsqrt(var + NORM_EPS)
    B, T, H, dv = y.shape
    return yn.reshape(B, T, H * dv) * g.astype(jnp.float32)


def token_mixers(h, cos, sin, w_in, q_norm, k_norm, dec_f, dec_b, ret_norm,
                 w_branch_attn, w_branch_ret, w_gate, b_gate, w_out):
    B, T, _ = h.shape
    proj = h @ w_in
    offs = [ATTN_Q_W, ATTN_Q_W + ATTN_KV_W, ATTN_Q_W + 2 * ATTN_KV_W,
            ATTN_Q_W + 2 * ATTN_KV_W + RET_QK_W, ATTN_Q_W + 2 * ATTN_KV_W + 2 * RET_QK_W,
            ATTN_Q_W + 2 * ATTN_KV_W + 2 * RET_QK_W + RET_V_W]
    aq, ak, av, rq, rk, rv, rg = jnp.split(proj, offs, axis=-1)
    aq = apply_rope(rmsnorm(aq.reshape(B, T, ATTN_HEADS, HEAD_DIM), q_norm), cos, sin)
    ak = apply_rope(rmsnorm(ak.reshape(B, T, ATTN_KV_HEADS, HEAD_DIM), k_norm), cos, sin)
    av = av.reshape(B, T, ATTN_KV_HEADS, HEAD_DIM)
    ya = gqa_block_attention(aq, ak, av) @ w_branch_attn
    rq = apply_rope(rq.reshape(B, T, RET_HEADS, RET_KEY_DIM), cos, sin)
    rk = apply_rope(rk.reshape(B, T, RET_HEADS, RET_KEY_DIM), cos, sin) * (RET_KEY_DIM ** -0.5)
    rv = rv.reshape(B, T, RET_HEADS, RET_VALUE_DIM)
    yr = head_group_norm(bidirectional_retention(rq, rk, rv, dec_f, dec_b), ret_norm).astype(h.dtype)
    yr = (jax.nn.silu(rg) * yr) @ w_branch_ret
    gates = jax.nn.sigmoid((h @ w_gate + b_gate).astype(jnp.float32)).astype(h.dtype)
    g_a, g_r = jnp.split(gates, 2, axis=-1)
    return (g_a * ya + g_r * yr) @ w_out


def trunk(x, ffn1_norm, ffn1_w13, ffn1_w2, mix_norm, w_in, q_norm, k_norm,
          ret_decay_fwd, ret_decay_bwd, ret_norm, w_branch_attn, w_branch_ret,
          w_gate, b_gate, w_out, ffn2_norm, ffn2_w13, ffn2_w2, final_norm):
    T = x.shape[1]
    cos, sin = axial_rope_tables(T)
    for l in range(DEPTH):
        x = x + 0.5 * swiglu_ffn(x, ffn1_norm[l], ffn1_w13[l], ffn1_w2[l])
        h = rmsnorm(x, mix_norm[l])
        x = x + token_mixers(h, cos, sin, w_in[l], q_norm[l], k_norm[l], ret_decay_fwd[l],
                             ret_decay_bwd[l], ret_norm[l], w_branch_attn[l], w_branch_ret[l],
                             w_gate[l], b_gate[l], w_out[l])
        x = x + 0.5 * swiglu_ffn(x, ffn2_norm[l], ffn2_w13[l], ffn2_w2[l])
    return rmsnorm(x, final_norm)


def setup_inputs(seed: int = 0) -> dict:
    key = jax.random.key(seed)
    ks = jax.random.split(key, 24)

    def nrm(k, shape, scale):
        return jax.random.normal(k, shape, jnp.float32) * scale

    base_decay = -(5.0 + jnp.arange(RET_HEADS, dtype=jnp.float32)) * math.log(2.0)
    return {
        'x_prompt': nrm(ks[0], (BATCH, SEQ, D_MODEL), 1.0),
        'x_sample': nrm(ks[1], (DEC_BATCH, DEC_SEQ, D_MODEL), 1.0),
        'ffn1_norm': 1.0 + nrm(ks[2], (DEPTH, D_MODEL), 0.02),
        'ffn1_w13': nrm(ks[3], (DEPTH, D_MODEL, 2 * D_FF), D_MODEL ** -0.5),
        'ffn1_w2': nrm(ks[4], (DEPTH, D_FF, D_MODEL), D_FF ** -0.5),
        'mix_norm': 1.0 + nrm(ks[5], (DEPTH, D_MODEL), 0.02),
        'w_in': nrm(ks[6], (DEPTH, D_MODEL, IN_PROJ_W), D_MODEL ** -0.5),
        'q_norm': 1.0 + nrm(ks[7], (DEPTH, HEAD_DIM), 0.02),
        'k_norm': 1.0 + nrm(ks[8], (DEPTH, HEAD_DIM), 0.02),
        'ret_decay_fwd': base_decay + nrm(ks[9], (DEPTH, RET_HEADS), 0.05),
        'ret_decay_bwd': base_decay + nrm(ks[10], (DEPTH, RET_HEADS), 0.05),
        'ret_norm': 1.0 + nrm(ks[11], (DEPTH, RET_V_W), 0.02),
        'w_branch_attn': nrm(ks[12], (DEPTH, ATTN_Q_W, D_MODEL), ATTN_Q_W ** -0.5),
        'w_branch_ret': nrm(ks[13], (DEPTH, RET_V_W, D_MODEL), RET_V_W ** -0.5),
        'w_gate': nrm(ks[14], (DEPTH, D_MODEL, 2 * D_MODEL), D_MODEL ** -0.5),
        'b_gate': nrm(ks[15], (DEPTH, 2 * D_MODEL), 0.02),
        'w_out': nrm(ks[16], (DEPTH, D_MODEL, D_MODEL), D_MODEL ** -0.5),
        'ffn2_norm': 1.0 + nrm(ks[17], (DEPTH, D_MODEL), 0.02),
        'ffn2_w13': nrm(ks[18], (DEPTH, D_MODEL, 2 * D_FF), D_MODEL ** -0.5),
        'ffn2_w2': nrm(ks[19], (DEPTH, D_FF, D_MODEL), D_FF ** -0.5),
        'final_norm': 1.0 + nrm(ks[20], (D_MODEL,), 0.02),
    }


def reference(x_prompt, x_sample, ffn1_norm, ffn1_w13, ffn1_w2, mix_norm, w_in, q_norm, k_norm,
              ret_decay_fwd, ret_decay_bwd, ret_norm, w_branch_attn, w_branch_ret,
              w_gate, b_gate, w_out, ffn2_norm, ffn2_w13, ffn2_w2, final_norm):
    y_prompt = trunk(x_prompt, ffn1_norm, ffn1_w13, ffn1_w2, mix_norm, w_in, q_norm, k_norm,
                     ret_decay_fwd, ret_decay_bwd, ret_norm, w_branch_attn, w_branch_ret,
                     w_gate, b_gate, w_out, ffn2_norm, ffn2_w13, ffn2_w2, final_norm)
    y_sample = trunk(x_sample, ffn1_norm, ffn1_w13, ffn1_w2, mix_norm, w_in, q_norm, k_norm,
                     ret_decay_fwd, ret_decay_bwd, ret_norm, w_branch_attn, w_branch_ret,
                     w_gate, b_gate, w_out, ffn2_norm, ffn2_w13, ffn2_w2, final_norm)
    return (y_prompt, y_sample)
```

```python
import functools
import math

import jax
import jax.numpy as jnp
from jax import lax
from jax.experimental import pallas as pl
from jax.experimental.pallas import tpu as pltpu

D_MODEL = 1024
GRID_W = 64
HEAD_DIM = 64
ATTN_HEADS = 8
ATTN_KV_HEADS = 2
ATTN_GROUP = ATTN_HEADS // ATTN_KV_HEADS
RET_HEADS = 4
RET_KEY_DIM = HEAD_DIM
RET_VALUE_DIM = 2 * RET_KEY_DIM
D_FF = 2816
ROPE_THETA = 10000.0
ROPE_AXIS_PAIRS = HEAD_DIM // 4
NORM_EPS = 1e-6
ATTN_Q_W = ATTN_HEADS * HEAD_DIM
ATTN_KV_W = ATTN_KV_HEADS * HEAD_DIM
RET_QK_W = RET_HEADS * RET_KEY_DIM
RET_V_W = RET_HEADS * RET_VALUE_DIM
IN_PROJ_W = ATTN_Q_W + 2 * ATTN_KV_W + 2 * RET_QK_W + 2 * RET_V_W

LANES = 128
MXU_TILE = 256
VMEM_LIMIT = 56 * 1024 * 1024

TOKEN_TILE = 512
FF_CHUNK = MXU_TILE
ATTN_TQ = 256
ATTN_TK = 512
RET_CHUNK = 128
ROPE_ROWS = 1024

_BF16 = jnp.bfloat16
_F32 = jnp.float32
_NT = (((1,), (1,)), ((), ()))
_TN = (((0,), (0,)), ((), ()))


def _vmem_spec():
    return pl.BlockSpec(memory_space=pltpu.VMEM)


def _params(*sem):
    return pltpu.CompilerParams(dimension_semantics=sem, vmem_limit_bytes=VMEM_LIMIT)


def _rms_rows(x, gain):
    ms = jnp.mean(x * x, axis=-1, keepdims=True)
    return x * lax.rsqrt(ms + NORM_EPS) * gain


def _rope_table_kernel(freq_ref, cos_ref, sin_ref):
    rows = cos_ref.shape[0]
    t = pl.program_id(0) * rows + lax.broadcasted_iota(jnp.int32, (rows, LANES), 0)
    lane = lax.broadcasted_iota(jnp.int32, (rows, LANES), 1)
    pair = lane & (2 * ROPE_AXIS_PAIRS - 1)
    shift = GRID_W.bit_length() - 1
    pos = jnp.where(pair < ROPE_AXIS_PAIRS, t >> shift, t & (GRID_W - 1)).astype(_F32)
    ang = pos * freq_ref[...]
    sign = jnp.where((lane & (HEAD_DIM - 1)) < HEAD_DIM // 2, -1.0, 1.0).astype(_F32)
    cos_ref[...] = jnp.cos(ang)
    sin_ref[...] = jnp.sin(ang) * sign


def _rope_tables(t_max):
    assert GRID_W & (GRID_W - 1) == 0 and t_max % ROPE_ROWS == 0
    freqs = ROPE_THETA ** (-jnp.arange(ROPE_AXIS_PAIRS, dtype=_F32) / ROPE_AXIS_PAIRS)
    freq_lanes = jnp.tile(freqs, LANES // ROPE_AXIS_PAIRS)[None, :]
    return pl.pallas_call(
        _rope_table_kernel,
        grid=(t_max // ROPE_ROWS,),
        in_specs=[pl.BlockSpec((1, LANES), lambda i: (0, 0))],
        out_specs=[pl.BlockSpec((ROPE_ROWS, LANES), lambda i: (i, 0))] * 2,
        out_shape=[jax.ShapeDtypeStruct((t_max, LANES), _F32)] * 2,
        compiler_params=_params("parallel"),
        name="rope_tables",
    )(freq_lanes)


def _ffn_kernel(x_ref, g_ref, w13_ref, w2_ref, fg_ref, o_ref, acc_ref, *, final_norm):
    x = x_ref[...]
    h = _rms_rows(x, g_ref[...]).astype(_BF16)
    for c in range(D_FF // FF_CHUNK):
        lo = c * FF_CHUNK
        a = jnp.dot(h, w13_ref[:, lo:lo + FF_CHUNK], preferred_element_type=_F32)
        b = jnp.dot(h, w13_ref[:, D_FF + lo:D_FF + lo + FF_CHUNK], preferred_element_type=_F32)
        act = (jax.nn.silu(a) * b).astype(_BF16)
        part = jnp.dot(act, w2_ref[lo:lo + FF_CHUNK, :], preferred_element_type=_F32)
        if c == 0:
            acc_ref[...] = part
        else:
            acc_ref[...] += part
    y = x + 0.5 * acc_ref[...]
    if final_norm:
        y = _rms_rows(y, fg_ref[...])
    o_ref[...] = y


def _ffn(x, gain, w13, w2, final_gain, final_norm):
    n, d = x.shape
    assert n % TOKEN_TILE == 0 and D_FF % FF_CHUNK == 0
    row = pl.BlockSpec((TOKEN_TILE, d), lambda i: (i, 0))
    vec = pl.BlockSpec((1, d), lambda i: (0, 0))
    return pl.pallas_call(
        functools.partial(_ffn_kernel, final_norm=final_norm),
        grid=(n // TOKEN_TILE,),
        in_specs=[row, vec, _vmem_spec(), _vmem_spec(), vec],
        out_specs=row,
        out_shape=jax.ShapeDtypeStruct((n, d), _F32),
        scratch_shapes=[pltpu.VMEM((TOKEN_TILE, d), _F32)],
        compiler_params=_params("parallel"),
        name="ffn",
    )(x, gain, w13, w2, final_gain)


def _segment_sum(sq, seg):
    hi = sq.astype(_BF16)
    lo = (sq - hi.astype(_F32)).astype(_BF16)
    return (jnp.dot(hi, seg, preferred_element_type=_F32)
            + jnp.dot(lo, seg, preferred_element_type=_F32))


def _rope_lanes(x, cos, sin, first_half):
    partner = jnp.where(first_half, pltpu.roll(x, LANES - HEAD_DIM // 2, axis=1),
                        pltpu.roll(x, HEAD_DIM // 2, axis=1))
    return x * cos + partner * sin


def _inproj_kernel(x_ref, g_ref, w_ref, qg_ref, kg_ref, seg_ref, cos_ref, sin_ref,
                   q_ref, k_ref, v_ref, rq_ref, rk_ref, rv_ref, rg_ref):
    h = _rms_rows(x_ref[...], g_ref[...]).astype(_BF16)
    proj = jnp.dot(h, w_ref[...], preferred_element_type=_F32)
    cos = cos_ref[...]
    sin = sin_ref[...]
    lane = lax.broadcasted_iota(jnp.int32, cos.shape, 1)
    first_half = (lane & (HEAD_DIM - 1)) < HEAD_DIM // 2
    seg = seg_ref[...]
    inv_hd = 1.0 / HEAD_DIM

    def head_norm_rope(xcols, gain, seg_block):
        ss = _segment_sum(xcols * xcols, seg_block)
        xn = xcols * lax.rsqrt(ss * inv_hd + NORM_EPS) * gain
        return [_rope_lanes(xn[:, j:j + LANES], cos, sin, first_half)
                for j in range(0, xn.shape[1], LANES)]

    o = 0
    aq = proj[:, o:o + ATTN_Q_W]; o += ATTN_Q_W
    ak = proj[:, o:o + ATTN_KV_W]; o += ATTN_KV_W
    av = proj[:, o:o + ATTN_KV_W]; o += ATTN_KV_W
    rq = proj[:, o:o + RET_QK_W]; o += RET_QK_W
    rk = proj[:, o:o + RET_QK_W]; o += RET_QK_W
    rv = proj[:, o:o + RET_V_W]; o += RET_V_W
    rg = proj[:, o:o + RET_V_W]

    scale = HEAD_DIM ** -0.5
    heads_per_block = LANES // HEAD_DIM
    qcols = []
    for j in range(0, ATTN_Q_W, MXU_TILE):
        qcols += head_norm_rope(aq[:, j:j + MXU_TILE], qg_ref[:, j:j + MXU_TILE], seg)
    for j, col in enumerate(qcols):
        col = (col * scale).astype(_BF16)
        for s in range(heads_per_block):
            q_ref[heads_per_block * j + s] = col[:, s * HEAD_DIM:(s + 1) * HEAD_DIM]
    kcols = head_norm_rope(ak, kg_ref[...], seg[:ATTN_KV_W, :ATTN_KV_W])
    for j, col in enumerate(kcols):
        col = col.astype(_BF16)
        for s in range(heads_per_block):
            k_ref[heads_per_block * j + s] = col[:, s * HEAD_DIM:(s + 1) * HEAD_DIM]
    avb = av.astype(_BF16)
    for g in range(ATTN_KV_HEADS):
        v_ref[g] = avb[:, g * HEAD_DIM:(g + 1) * HEAD_DIM]

    for j in range(0, RET_QK_W, LANES):
        rqj = _rope_lanes(rq[:, j:j + LANES], cos, sin, first_half)
        rkj = _rope_lanes(rk[:, j:j + LANES], cos, sin, first_half) * (RET_KEY_DIM ** -0.5)
        for s in range(heads_per_block):
            hidx = heads_per_block * (j // LANES) + s
            rq_ref[hidx] = rqj[:, s * HEAD_DIM:(s + 1) * HEAD_DIM]
            rk_ref[hidx] = rkj[:, s * HEAD_DIM:(s + 1) * HEAD_DIM]
    rv_ref[...] = rv.astype(_BF16)
    rg_ref[...] = rg


def _inproj(x, gain, w_in, q_gain, k_gain, seg, cos, sin, seq_len):
    n, d = x.shape
    tm = TOKEN_TILE
    assert n % tm == 0 and seq_len % tm == 0
    row = lambda w: pl.BlockSpec((tm, w), lambda i: (i, 0))
    vec = lambda w: pl.BlockSpec((1, w), lambda i: (0, 0))
    heads = lambda nh: pl.BlockSpec((nh, tm, HEAD_DIM), lambda i: (0, i, 0))
    table = pl.BlockSpec((tm, LANES), lambda i: (i % (seq_len // tm), 0))
    return pl.pallas_call(
        _inproj_kernel,
        grid=(n // tm,),
        in_specs=[row(d), vec(d), _vmem_spec(), vec(ATTN_Q_W), vec(ATTN_KV_W), _vmem_spec(),
                  table, table],
        out_specs=[heads(ATTN_HEADS), heads(ATTN_KV_HEADS), heads(ATTN_KV_HEADS),
                   heads(RET_HEADS), heads(RET_HEADS), row(RET_V_W), row(RET_V_W)],
        out_shape=[jax.ShapeDtypeStruct((ATTN_HEADS, n, HEAD_DIM), _BF16),
                   jax.ShapeDtypeStruct((ATTN_KV_HEADS, n, HEAD_DIM), _BF16),
                   jax.ShapeDtypeStruct((ATTN_KV_HEADS, n, HEAD_DIM), _BF16),
                   jax.ShapeDtypeStruct((RET_HEADS, n, RET_KEY_DIM), _F32),
                   jax.ShapeDtypeStruct((RET_HEADS, n, RET_KEY_DIM), _F32),
                   jax.ShapeDtypeStruct((n, RET_V_W), _BF16),
                   jax.ShapeDtypeStruct((n, RET_V_W), _F32)],
        compiler_params=_params("parallel"),
        name="inproj",
    )(x, gain, w_in, q_gain, k_gain, seg, cos, sin)


def _attn_kernel(q_ref, k_ref, v_ref, o_ref, m_ref, l_ref, acc_ref, *, seq_len):
    q = q_ref[...].reshape(ATTN_GROUP * ATTN_TQ, HEAD_DIM)
    m_ref[...] = jnp.full(m_ref.shape, -jnp.inf, _F32)
    l_ref[...] = jnp.zeros(l_ref.shape, _F32)
    acc_ref[...] = jnp.zeros(acc_ref.shape, _F32)

    def step(j, carry):
        start = pl.multiple_of(j * ATTN_TK, ATTN_TK)
        kj = k_ref[0, pl.ds(start, ATTN_TK), :]
        vj = v_ref[0, pl.ds(start, ATTN_TK), :]
        s = lax.dot_general(q, kj, _NT, preferred_element_type=_F32)
        m_prev = m_ref[...]
        m_next = jnp.maximum(m_prev, jnp.max(s, axis=1, keepdims=True))
        p = jnp.exp(s - pltpu.repeat(m_next, ATTN_TK // LANES, axis=1))
        alpha = jnp.exp(m_prev - m_next)
        l_ref[...] = alpha * l_ref[...] + jnp.sum(p, axis=1, keepdims=True)
        m_ref[...] = m_next
        pv = jnp.dot(p.astype(_BF16), vj, preferred_element_type=_F32)
        acc_ref[...] = alpha[:, :HEAD_DIM] * acc_ref[...] + pv
        return carry

    lax.fori_loop(0, seq_len // ATTN_TK, step, 0)
    out = acc_ref[...] / l_ref[...][:, :HEAD_DIM]
    for g in range(ATTN_GROUP):
        o_ref[:, g * HEAD_DIM:(g + 1) * HEAD_DIM] = (
            out[g * ATTN_TQ:(g + 1) * ATTN_TQ].astype(o_ref.dtype))


def _attention(q, k, v, batch, seq_len):
    n = q.shape[1]
    assert n == batch * seq_len and seq_len % ATTN_TK == 0 and seq_len % ATTN_TQ == 0
    nq = seq_len // ATTN_TQ
    rows = ATTN_GROUP * ATTN_TQ
    return pl.pallas_call(
        functools.partial(_attn_kernel, seq_len=seq_len),
        grid=(batch, ATTN_KV_HEADS, nq),
        in_specs=[
            pl.BlockSpec((ATTN_GROUP, ATTN_TQ, HEAD_DIM), lambda b, g, i: (g, b * nq + i, 0)),
            pl.BlockSpec((1, seq_len, HEAD_DIM), lambda b, g, i: (g, b, 0)),
            pl.BlockSpec((1, seq_len, HEAD_DIM), lambda b, g, i: (g, b, 0)),
        ],
        out_specs=pl.BlockSpec((ATTN_TQ, ATTN_GROUP * HEAD_DIM), lambda b, g, i: (b * nq + i, g)),
        out_shape=jax.ShapeDtypeStruct((n, ATTN_Q_W), _BF16),
        scratch_shapes=[pltpu.VMEM((rows, LANES), _F32), pltpu.VMEM((rows, LANES), _F32),
                        pltpu.VMEM((rows, HEAD_DIM), _F32)],
        compiler_params=_params("parallel", "parallel", "arbitrary"),
        name=f"attn_t{seq_len}",
    )(q, k, v)


def _ret_kernel(dec_ref, qf_ref, kf_ref, vf_ref, qb_ref, kb_ref, vb_ref, of_ref, ob_ref,
                dmat_ref, qd_ref, kd_ref, cd_ref, s_ref):
    c = RET_CHUNK
    t = pl.program_id(1)

    @pl.when(t == 0)
    def _init():
        row = lax.broadcasted_iota(jnp.int32, (c, c), 0).astype(_F32)
        col = lax.broadcasted_iota(jnp.int32, (c, c), 1).astype(_F32)
        idx = lax.broadcasted_iota(jnp.int32, (c, LANES), 0).astype(_F32)
        for slot in range(2 * RET_HEADS):
            fwd = slot < RET_HEADS
            lg_c = jnp.log1p(-jnp.exp(dec_ref[slot:slot + 1, :]))
            lg = lg_c[:, :LANES]
            diff = (row - col) if fwd else (col - row)
            mask = (diff >= 0) if fwd else (diff > 0)
            dmat_ref[slot] = jnp.where(mask, jnp.exp(jnp.where(mask, diff, 0.0) * lg_c), 0.0)
            qd_ref[slot] = jnp.exp(((idx + 1.0) if fwd else (c - idx)) * lg)
            kd_ref[slot] = jnp.exp(((c - 1.0 - idx) if fwd else idx) * lg)
            cd_ref[slot] = jnp.exp(c * lg) * jnp.ones((8, LANES), _F32)
        s_ref[...] = jnp.zeros(s_ref.shape, _F32)

    for slot in range(2 * RET_HEADS):
        fwd = slot < RET_HEADS
        hd = slot % RET_HEADS
        q = (qf_ref if fwd else qb_ref)[hd]
        k = (kf_ref if fwd else kb_ref)[hd]
        v = (vf_ref if fwd else vb_ref)[:, hd * RET_VALUE_DIM:(hd + 1) * RET_VALUE_DIM]
        state = s_ref[slot]
        inner = lax.dot_general(q.astype(_BF16), k.astype(_BF16), _NT,
                                preferred_element_type=_F32) * dmat_ref[slot]
        o = jnp.dot(inner.astype(_BF16), v, preferred_element_type=_F32)
        qs = (q * qd_ref[slot][:, :RET_KEY_DIM]).astype(_BF16)
        o = o + jnp.dot(qs, state.astype(_BF16), preferred_element_type=_F32)
        ks = (k * kd_ref[slot][:, :RET_KEY_DIM]).astype(_BF16)
        s_ref[slot] = state * cd_ref[slot][:1, :] + lax.dot_general(
            ks, v, _TN, preferred_element_type=_F32)
        out_ref = of_ref if fwd else ob_ref
        out_ref[:, hd * RET_VALUE_DIM:(hd + 1) * RET_VALUE_DIM] = o


def _retention(rq, rk, rv, dec, batch, seq_len):
    n = rq.shape[1]
    c = RET_CHUNK
    assert n == batch * seq_len and seq_len % c == 0 and RET_VALUE_DIM == LANES
    nc = seq_len // c
    fwd = lambda b, t: b * nc + t
    bwd = lambda b, t: b * nc + (nc - 1 - t)
    qk = lambda f: pl.BlockSpec((RET_HEADS, c, RET_KEY_DIM), lambda b, t: (0, f(b, t), 0))
    vv = lambda f: pl.BlockSpec((c, RET_V_W), lambda b, t: (f(b, t), 0))
    slots = 2 * RET_HEADS
    return pl.pallas_call(
        _ret_kernel,
        grid=(batch, nc),
        in_specs=[pl.BlockSpec((slots, c), lambda b, t: (0, 0)),
                  qk(fwd), qk(fwd), vv(fwd), qk(bwd), qk(bwd), vv(bwd)],
        out_specs=[vv(fwd), vv(bwd)],
        out_shape=[jax.ShapeDtypeStruct((n, RET_V_W), _F32)] * 2,
        scratch_shapes=[pltpu.VMEM((slots, c, c), _F32),
                        pltpu.VMEM((slots, c, LANES), _F32),
                        pltpu.VMEM((slots, c, LANES), _F32),
                        pltpu.VMEM((slots, 8, LANES), _F32),
                        pltpu.VMEM((slots, RET_KEY_DIM, RET_VALUE_DIM), _F32)],
        compiler_params=_params("arbitrary", "arbitrary"),
        name=f"retention_t{seq_len}",
    )(dec, rq, rk, rv, rq, rk, rv)


def _merge_kernel(x_ref, g_ref, ao_ref, rf_ref, rb_ref, rg_ref, rn_ref, wg_ref, bg_ref,
                  wba_ref, wbr_ref, wo_ref, o_ref):
    x = x_ref[...]
    h = _rms_rows(x, g_ref[...]).astype(_BF16)
    gates = jax.nn.sigmoid(jnp.dot(h, wg_ref[...], preferred_element_type=_F32) + bg_ref[...])
    ya = jnp.dot(ao_ref[...], wba_ref[...], preferred_element_type=_F32)
    y = rf_ref[...] + rb_ref[...]
    cols = []
    for j in range(0, RET_V_W, RET_VALUE_DIM):
        yh = y[:, j:j + RET_VALUE_DIM]
        mu = jnp.mean(yh, axis=-1, keepdims=True)
        yc = yh - mu
        var = jnp.mean(yc * yc, axis=-1, keepdims=True)
        cols.append(yc * lax.rsqrt(var + NORM_EPS))
    yn = jnp.concatenate(cols, axis=1) * rn_ref[...]
    yr_in = (jax.nn.silu(rg_ref[...]) * yn).astype(_BF16)
    yr = jnp.dot(yr_in, wbr_ref[...], preferred_element_type=_F32)
    mixed = (gates[:, :D_MODEL] * ya + gates[:, D_MODEL:] * yr).astype(_BF16)
    o_ref[...] = x + jnp.dot(mixed, wo_ref[...], preferred_element_type=_F32)


def _merge(x, gain, ao, rf, rb, rg, ret_gain, w_gate, b_gate, w_ba, w_br, w_out):
    n, d = x.shape
    tm = TOKEN_TILE
    row = lambda w: pl.BlockSpec((tm, w), lambda i: (i, 0))
    vec = lambda w: pl.BlockSpec((1, w), lambda i: (0, 0))
    return pl.pallas_call(
        _merge_kernel,
        grid=(n // tm,),
        in_specs=[row(d), vec(d), row(ATTN_Q_W), row(RET_V_W), row(RET_V_W), row(RET_V_W),
                  vec(RET_V_W), _vmem_spec(), vec(2 * d), _vmem_spec(), _vmem_spec(), _vmem_spec()],
        out_specs=row(d),
        out_shape=jax.ShapeDtypeStruct((n, d), _F32),
        compiler_params=_params("parallel"),
        name="merge",
    )(x, gain, ao, rf, rb, rg, ret_gain, w_gate, b_gate, w_ba, w_br, w_out)


def _trunk(xs, p):
    depth = p["ffn1_w13"].shape[0]
    shapes = [x.shape[:2] for x in xs]
    xs = [x.reshape(-1, D_MODEL) for x in xs]
    cos, sin = _rope_tables(max(t for _, t in shapes))

    seg_id = jnp.arange(MXU_TILE) // HEAD_DIM
    seg = (seg_id[:, None] == seg_id[None, :]).astype(_BF16)
    bf = lambda w: w.astype(_BF16)
    vec = lambda g: g.astype(_F32)[None, :]
    final_gain = vec(p["final_norm"])

    for l in range(depth):
        ffn1 = (vec(p["ffn1_norm"][l]), bf(p["ffn1_w13"][l]), bf(p["ffn1_w2"][l]), final_gain)
        ffn2 = (vec(p["ffn2_norm"][l]), bf(p["ffn2_w13"][l]), bf(p["ffn2_w2"][l]), final_gain)
        mix_gain = vec(p["mix_norm"][l])
        proj = (mix_gain, bf(p["w_in"][l]), vec(jnp.tile(p["q_norm"][l], ATTN_HEADS)),
                vec(jnp.tile(p["k_norm"][l], ATTN_KV_HEADS)), seg, cos, sin)
        dec = jnp.concatenate([p["ret_decay_fwd"][l], p["ret_decay_bwd"][l]]).astype(_F32)
        dec = jnp.broadcast_to(dec[:, None], (2 * RET_HEADS, RET_CHUNK))
        merge = (vec(p["ret_norm"][l]), bf(p["w_gate"][l]), vec(p["b_gate"][l]),
                 bf(p["w_branch_attn"][l]), bf(p["w_branch_ret"][l]), bf(p["w_out"][l]))
        for i, (b, t) in enumerate(shapes):
            x = _ffn(xs[i], *ffn1, False)
            q, k, v, rq, rk, rv, rg = _inproj(x, *proj, t)
            ao = _attention(q, k, v, b, t)
            rf, rb = _retention(rq, rk, rv, dec, b, t)
            x = _merge(x, mix_gain, ao, rf, rb, rg, *merge)
            xs[i] = _ffn(x, *ffn2, l == depth - 1)

    return [x.reshape(b, t, D_MODEL) for x, (b, t) in zip(xs, shapes)]


def kernel(x_prompt, x_sample, ffn1_norm, ffn1_w13, ffn1_w2, mix_norm, w_in, q_norm, k_norm,
           ret_decay_fwd, ret_decay_bwd, ret_norm, w_branch_attn, w_branch_ret, w_gate, b_gate,
           w_out, ffn2_norm, ffn2_w13, ffn2_w2, final_norm):
    p = dict(ffn1_norm=ffn1_norm, ffn1_w13=ffn1_w13, ffn1_w2=ffn1_w2, mix_norm=mix_norm, w_in=w_in,
             q_norm=q_norm, k_norm=k_norm, ret_decay_fwd=ret_decay_fwd, ret_decay_bwd=ret_decay_bwd,
             ret_norm=ret_norm, w_branch_attn=w_branch_attn, w_branch_ret=w_branch_ret, w_gate=w_gate,
             b_gate=b_gate, w_out=w_out, ffn2_norm=ffn2_norm, ffn2_w13=ffn2_w13, ffn2_w2=ffn2_w2,
             final_norm=final_norm)
    y_prompt, y_sample = _trunk([x_prompt, x_sample], p)
    return (y_prompt, y_sample)
```

```python
import functools
import math

import jax
import jax.numpy as jnp
from jax import lax
from jax.experimental import pallas as pl
from jax.experimental.pallas import tpu as pltpu

D_MODEL = 1024
GRID_W = 64
HEAD_DIM = 64
ATTN_HEADS = 8
ATTN_KV_HEADS = 2
ATTN_GROUP = ATTN_HEADS // ATTN_KV_HEADS
RET_HEADS = 4
RET_KEY_DIM = HEAD_DIM
RET_VALUE_DIM = 2 * RET_KEY_DIM
D_FF = 2816
ROPE_THETA = 10000.0
ROPE_AXIS_PAIRS = HEAD_DIM // 4
NORM_EPS = 1e-6
ATTN_Q_W = ATTN_HEADS * HEAD_DIM
ATTN_KV_W = ATTN_KV_HEADS * HEAD_DIM
RET_QK_W = RET_HEADS * RET_KEY_DIM
RET_V_W = RET_HEADS * RET_VALUE_DIM
IN_PROJ_W = ATTN_Q_W + 2 * ATTN_KV_W + 2 * RET_QK_W + 2 * RET_V_W

LANES = 128
MXU_TILE = 256
VMEM_LIMIT = 56 * 1024 * 1024

TOKEN_TILE = 512
FF_CHUNK = MXU_TILE
ATTN_TQ = 256
ATTN_TK = 512
VT_ROWS = HEAD_DIM + 16
RET_CHUNK = 128
ROPE_ROWS = 1024

_BF16 = jnp.bfloat16
_F32 = jnp.float32
_NT = (((1,), (1,)), ((), ()))
_TN = (((0,), (0,)), ((), ()))


def _vmem_spec():
    return pl.BlockSpec(memory_space=pltpu.VMEM)


def _params(*sem):
    return pltpu.CompilerParams(dimension_semantics=sem, vmem_limit_bytes=VMEM_LIMIT)


def _rms_rows(x, gain):
    ms = jnp.mean(x * x, axis=-1, keepdims=True)
    return x * lax.rsqrt(ms + NORM_EPS) * gain


def _rope_table_kernel(freq_ref, cos_ref, sin_ref):
    rows = cos_ref.shape[0]
    t = pl.program_id(0) * rows + lax.broadcasted_iota(jnp.int32, (rows, LANES), 0)
    lane = lax.broadcasted_iota(jnp.int32, (rows, LANES), 1)
    pair = lane & (2 * ROPE_AXIS_PAIRS - 1)
    shift = GRID_W.bit_length() - 1
    pos = jnp.where(pair < ROPE_AXIS_PAIRS, t >> shift, t & (GRID_W - 1)).astype(_F32)
    ang = pos * freq_ref[...]
    sign = jnp.where((lane & (HEAD_DIM - 1)) < HEAD_DIM // 2, -1.0, 1.0).astype(_F32)
    cos_ref[...] = jnp.cos(ang)
    sin_ref[...] = jnp.sin(ang) * sign


def _rope_tables(t_max):
    assert GRID_W & (GRID_W - 1) == 0 and t_max % ROPE_ROWS == 0
    freqs = ROPE_THETA ** (-jnp.arange(ROPE_AXIS_PAIRS, dtype=_F32) / ROPE_AXIS_PAIRS)
    freq_lanes = jnp.tile(freqs, LANES // ROPE_AXIS_PAIRS)[None, :]
    return pl.pallas_call(
        _rope_table_kernel,
        grid=(t_max // ROPE_ROWS,),
        in_specs=[pl.BlockSpec((1, LANES), lambda i: (0, 0))],
        out_specs=[pl.BlockSpec((ROPE_ROWS, LANES), lambda i: (i, 0))] * 2,
        out_shape=[jax.ShapeDtypeStruct((t_max, LANES), _F32)] * 2,
        compiler_params=_params("parallel"),
        name="rope_tables",
    )(freq_lanes)


def _ffn_kernel(x_ref, g_ref, w13_ref, w2_ref, fg_ref, o_ref, acc_ref, *, final_norm):
    x = x_ref[...]
    h = _rms_rows(x, g_ref[...]).astype(_BF16)
    for c in range(D_FF // FF_CHUNK):
        lo = c * FF_CHUNK
        a = jnp.dot(h, w13_ref[:, lo:lo + FF_CHUNK], preferred_element_type=_F32)
        b = jnp.dot(h, w13_ref[:, D_FF + lo:D_FF + lo + FF_CHUNK], preferred_element_type=_F32)
        act = (jax.nn.silu(a) * b).astype(_BF16)
        part = jnp.dot(act, w2_ref[lo:lo + FF_CHUNK, :], preferred_element_type=_F32)
        if c == 0:
            acc_ref[...] = part
        else:
            acc_ref[...] += part
    y = x + 0.5 * acc_ref[...]
    if final_norm:
        y = _rms_rows(y, fg_ref[...])
    o_ref[...] = y


def _ffn(x, gain, w13, w2, final_gain, final_norm):
    n, d = x.shape
    assert n % TOKEN_TILE == 0 and D_FF % FF_CHUNK == 0
    row = pl.BlockSpec((TOKEN_TILE, d), lambda i: (i, 0))
    vec = pl.BlockSpec((1, d), lambda i: (0, 0))
    return pl.pallas_call(
        functools.partial(_ffn_kernel, final_norm=final_norm),
        grid=(n // TOKEN_TILE,),
        in_specs=[row, vec, _vmem_spec(), _vmem_spec(), vec],
        out_specs=row,
        out_shape=jax.ShapeDtypeStruct((n, d), _F32),
        scratch_shapes=[pltpu.VMEM((TOKEN_TILE, d), _F32)],
        compiler_params=_params("parallel"),
        name="ffn",
    )(x, gain, w13, w2, final_gain)


def _segment_sum(sq, seg):
    hi = sq.astype(_BF16)
    lo = (sq - hi.astype(_F32)).astype(_BF16)
    return (jnp.dot(hi, seg, preferred_element_type=_F32)
            + jnp.dot(lo, seg, preferred_element_type=_F32))


def _rope_lanes(x, cos, sin, first_half):
    partner = jnp.where(first_half, pltpu.roll(x, LANES - HEAD_DIM // 2, axis=1),
                        pltpu.roll(x, HEAD_DIM // 2, axis=1))
    return x * cos + partner * sin


def _inproj_kernel(x_ref, g_ref, w_ref, qg_ref, kg_ref, seg_ref, cos_ref, sin_ref,
                   q_ref, k_ref, v_ref, rq_ref, rk_ref, rv_ref, rg_ref):
    h = _rms_rows(x_ref[...], g_ref[...]).astype(_BF16)
    proj = jnp.dot(h, w_ref[...], preferred_element_type=_F32)
    cos = cos_ref[...]
    sin = sin_ref[...]
    lane = lax.broadcasted_iota(jnp.int32, cos.shape, 1)
    first_half = (lane & (HEAD_DIM - 1)) < HEAD_DIM // 2
    seg = seg_ref[...]
    inv_hd = 1.0 / HEAD_DIM

    def head_norm_rope(xcols, gain, seg_block):
        ss = _segment_sum(xcols * xcols, seg_block)
        xn = xcols * lax.rsqrt(ss * inv_hd + NORM_EPS) * gain
        return [_rope_lanes(xn[:, j:j + LANES], cos, sin, first_half)
                for j in range(0, xn.shape[1], LANES)]

    o = 0
    aq = proj[:, o:o + ATTN_Q_W]; o += ATTN_Q_W
    ak = proj[:, o:o + ATTN_KV_W]; o += ATTN_KV_W
    av = proj[:, o:o + ATTN_KV_W]; o += ATTN_KV_W
    rq = proj[:, o:o + RET_QK_W]; o += RET_QK_W
    rk = proj[:, o:o + RET_QK_W]; o += RET_QK_W
    rv = proj[:, o:o + RET_V_W]; o += RET_V_W
    rg = proj[:, o:o + RET_V_W]

    scale = HEAD_DIM ** -0.5 * math.log2(math.e)
    heads_per_block = LANES // HEAD_DIM
    qcols = []
    for j in range(0, ATTN_Q_W, MXU_TILE):
        qcols += head_norm_rope(aq[:, j:j + MXU_TILE], qg_ref[:, j:j + MXU_TILE], seg)
    for j, col in enumerate(qcols):
        col_t = (col * scale).T.astype(_BF16)
        for s in range(heads_per_block):
            q_ref[heads_per_block * j + s] = col_t[s * HEAD_DIM:(s + 1) * HEAD_DIM]
    kcols = head_norm_rope(ak, kg_ref[...], seg[:ATTN_KV_W, :ATTN_KV_W])
    for j, col in enumerate(kcols):
        col = col.astype(_BF16)
        for s in range(heads_per_block):
            k_ref[heads_per_block * j + s] = col[:, s * HEAD_DIM:(s + 1) * HEAD_DIM]
    av_t = av.T.astype(_BF16)
    ones = jnp.ones((VT_ROWS - HEAD_DIM, av_t.shape[1]), _BF16)
    for g in range(ATTN_KV_HEADS):
        v_ref[g, 0, :HEAD_DIM] = av_t[g * HEAD_DIM:(g + 1) * HEAD_DIM]
        v_ref[g, 0, HEAD_DIM:] = ones

    for j in range(0, RET_QK_W, LANES):
        rqj = _rope_lanes(rq[:, j:j + LANES], cos, sin, first_half)
        rkj = _rope_lanes(rk[:, j:j + LANES], cos, sin, first_half) * (RET_KEY_DIM ** -0.5)
        for s in range(heads_per_block):
            hidx = heads_per_block * (j // LANES) + s
            rq_ref[hidx] = rqj[:, s * HEAD_DIM:(s + 1) * HEAD_DIM]
            rk_ref[hidx] = rkj[:, s * HEAD_DIM:(s + 1) * HEAD_DIM]
    rv_ref[...] = rv.astype(_BF16)
    rg_ref[...] = rg


def _inproj(x, gain, w_in, q_gain, k_gain, seg, cos, sin, seq_len):
    n, d = x.shape
    tm = TOKEN_TILE
    assert n % tm == 0 and seq_len % tm == 0 and tm == ATTN_TK
    row = lambda w: pl.BlockSpec((tm, w), lambda i: (i, 0))
    vec = lambda w: pl.BlockSpec((1, w), lambda i: (0, 0))
    heads = lambda nh: pl.BlockSpec((nh, tm, HEAD_DIM), lambda i: (0, i, 0))
    heads_t = pl.BlockSpec((ATTN_HEADS, HEAD_DIM, tm), lambda i: (0, 0, i))
    chunks_t = pl.BlockSpec((ATTN_KV_HEADS, 1, VT_ROWS, tm), lambda i: (0, i, 0, 0))
    table = pl.BlockSpec((tm, LANES), lambda i: (i % (seq_len // tm), 0))
    return pl.pallas_call(
        _inproj_kernel,
        grid=(n // tm,),
        in_specs=[row(d), vec(d), _vmem_spec(), vec(ATTN_Q_W), vec(ATTN_KV_W), _vmem_spec(),
                  table, table],
        out_specs=[heads_t, heads(ATTN_KV_HEADS), chunks_t,
                   heads(RET_HEADS), heads(RET_HEADS), row(RET_V_W), row(RET_V_W)],
        out_shape=[jax.ShapeDtypeStruct((ATTN_HEADS, HEAD_DIM, n), _BF16),
                   jax.ShapeDtypeStruct((ATTN_KV_HEADS, n, HEAD_DIM), _BF16),
                   jax.ShapeDtypeStruct((ATTN_KV_HEADS, n // tm, VT_ROWS, tm), _BF16),
                   jax.ShapeDtypeStruct((RET_HEADS, n, RET_KEY_DIM), _F32),
                   jax.ShapeDtypeStruct((RET_HEADS, n, RET_KEY_DIM), _F32),
                   jax.ShapeDtypeStruct((n, RET_V_W), _BF16),
                   jax.ShapeDtypeStruct((n, RET_V_W), _F32)],
        compiler_params=_params("parallel"),
        name="inproj",
    )(x, gain, w_in, q_gain, k_gain, seg, cos, sin)


def _attn_kernel(qt_ref, k_ref, vt_ref, o_ref, acc_ref, s0_ref, s1_ref, s2_ref, s3_ref, *, seq_len):
    qt = jnp.concatenate([qt_ref[h] for h in range(ATTN_GROUP)], axis=1)
    nk = seq_len // ATTN_TK
    acc_ref[...] = jnp.zeros(acc_ref.shape, _F32)

    def scores(j, dst_ref, m_prev):
        start = pl.multiple_of(j * ATTN_TK, ATTN_TK)
        kj = k_ref[0, pl.ds(start, ATTN_TK), :]
        st = jnp.dot(kj, qt, preferred_element_type=_F32)
        dst_ref[...] = st
        m_next = jnp.maximum(m_prev, jnp.max(st, axis=0, keepdims=True))
        return m_next, jnp.exp2(m_prev - m_next)

    def consume(j, src_ref, m, alpha):
        pt = jnp.exp2(src_ref[...] - m)
        pv = jnp.dot(vt_ref[0, j], pt.astype(_BF16), preferred_element_type=_F32)
        acc_ref[...] = alpha * acc_ref[...] + pv

    bufs = (s0_ref, s1_ref, s2_ref, s3_ref)
    m0, a0 = scores(0, bufs[0], jnp.full((1, qt.shape[1]), -jnp.inf, _F32))
    m1, a1 = scores(1, bufs[1], m0)

    def half_step(j, carry, cur, nxt, lookahead):
        m_a, alpha_a, m_b, alpha_b = carry
        if lookahead:
            m_c, alpha_c = scores(j + 2, nxt[0], m_b)
        consume(j, cur[0], m_a, alpha_a)
        if lookahead:
            m_d, alpha_d = scores(j + 3, nxt[1], m_c)
        consume(j + 1, cur[1], m_b, alpha_b)
        return (m_c, alpha_c, m_d, alpha_d) if lookahead else None

    def quad_step(i, carry):
        j = 4 * i
        carry = half_step(j, carry, bufs[:2], bufs[2:], True)
        return half_step(j + 2, carry, bufs[2:], bufs[:2], True)

    carry = lax.fori_loop(0, nk // 4 - 1, quad_step, (m0, a0, m1, a1))
    carry = half_step(nk - 4, carry, bufs[:2], bufs[2:], True)
    half_step(nk - 2, carry, bufs[2:], bufs[:2], False)
    out_t = acc_ref[:HEAD_DIM, :] / acc_ref[HEAD_DIM:HEAD_DIM + 1, :]
    pair = LANES // HEAD_DIM
    for c in range(ATTN_GROUP // pair):
        stacked = jnp.concatenate(
            [out_t[:, (pair * c + s) * ATTN_TQ:(pair * c + s + 1) * ATTN_TQ] for s in range(pair)],
            axis=0)
        o_ref[:, c * LANES:(c + 1) * LANES] = stacked.T.astype(o_ref.dtype)


def _attention(qt, k, vt, batch, seq_len):
    n = k.shape[1]
    assert n == batch * seq_len and seq_len % (4 * ATTN_TK) == 0 and seq_len % ATTN_TQ == 0
    nq = seq_len // ATTN_TQ
    nk = seq_len // ATTN_TK
    cols = ATTN_GROUP * ATTN_TQ
    return pl.pallas_call(
        functools.partial(_attn_kernel, seq_len=seq_len),
        grid=(batch, ATTN_KV_HEADS, nq),
        in_specs=[
            pl.BlockSpec((ATTN_GROUP, HEAD_DIM, ATTN_TQ), lambda b, g, i: (g, 0, b * nq + i)),
            pl.BlockSpec((1, seq_len, HEAD_DIM), lambda b, g, i: (g, b, 0)),
            pl.BlockSpec((1, nk, VT_ROWS, ATTN_TK), lambda b, g, i: (g, b, 0, 0)),
        ],
        out_specs=pl.BlockSpec((ATTN_TQ, ATTN_GROUP * HEAD_DIM), lambda b, g, i: (b * nq + i, g)),
        out_shape=jax.ShapeDtypeStruct((n, ATTN_Q_W), _BF16),
        scratch_shapes=[pltpu.VMEM((VT_ROWS, cols), _F32),
                        ] + [pltpu.VMEM((ATTN_TK, cols), _F32)] * 4,
        compiler_params=_params("parallel", "parallel", "arbitrary"),
        name=f"attn_t{seq_len}",
    )(qt, k, vt)


def _ret_kernel(dec_ref, qf_ref, kf_ref, vf_ref, qb_ref, kb_ref, vb_ref, of_ref, ob_ref,
                dmat_ref, qd_ref, kd_ref, cd_ref, s_ref):
    c = RET_CHUNK
    t = pl.program_id(1)

    @pl.when(t == 0)
    def _init():
        row = lax.broadcasted_iota(jnp.int32, (c, c), 0).astype(_F32)
        col = lax.broadcasted_iota(jnp.int32, (c, c), 1).astype(_F32)
        idx = lax.broadcasted_iota(jnp.int32, (c, LANES), 0).astype(_F32)
        for slot in range(2 * RET_HEADS):
            fwd = slot < RET_HEADS
            lg_c = jnp.log1p(-jnp.exp(dec_ref[slot:slot + 1, :]))
            lg = lg_c[:, :LANES]
            diff = (row - col) if fwd else (col - row)
            mask = (diff >= 0) if fwd else (diff > 0)
            dmat_ref[slot] = jnp.where(mask, jnp.exp(jnp.where(mask, diff, 0.0) * lg_c), 0.0)
            qd_ref[slot] = jnp.exp(((idx + 1.0) if fwd else (c - idx)) * lg)
            kd_ref[slot] = jnp.exp(((c - 1.0 - idx) if fwd else idx) * lg)
            cd_ref[slot] = jnp.exp(c * lg) * jnp.ones((8, LANES), _F32)
        s_ref[...] = jnp.zeros(s_ref.shape, _F32)

    for slot in range(2 * RET_HEADS):
        fwd = slot < RET_HEADS
        hd = slot % RET_HEADS
        q = (qf_ref if fwd else qb_ref)[hd]
        k = (kf_ref if fwd else kb_ref)[hd]
        v = (vf_ref if fwd else vb_ref)[:, hd * RET_VALUE_DIM:(hd + 1) * RET_VALUE_DIM]
        state = s_ref[slot]
        inner = lax.dot_general(q.astype(_BF16), k.astype(_BF16), _NT,
                                preferred_element_type=_F32) * dmat_ref[slot]
        o = jnp.dot(inner.astype(_BF16), v, preferred_element_type=_F32)
        qs = (q * qd_ref[slot][:, :RET_KEY_DIM]).astype(_BF16)
        o = o + jnp.dot(qs, state.astype(_BF16), preferred_element_type=_F32)
        ks = (k * kd_ref[slot][:, :RET_KEY_DIM]).astype(_BF16)
        s_ref[slot] = state * cd_ref[slot][:1, :] + lax.dot_general(
            ks, v, _TN, preferred_element_type=_F32)
        out_ref = of_ref if fwd else ob_ref
        out_ref[:, hd * RET_VALUE_DIM:(hd + 1) * RET_VALUE_DIM] = o


def _retention(rq, rk, rv, dec, batch, seq_len):
    n = rq.shape[1]
    c = RET_CHUNK
    assert n == batch * seq_len and seq_len % c == 0 and RET_VALUE_DIM == LANES
    nc = seq_len // c
    fwd = lambda b, t: b * nc + t
    bwd = lambda b, t: b * nc + (nc - 1 - t)
    qk = lambda f: pl.BlockSpec((RET_HEADS, c, RET_KEY_DIM), lambda b, t: (0, f(b, t), 0))
    vv = lambda f: pl.BlockSpec((c, RET_V_W), lambda b, t: (f(b, t), 0))
    slots = 2 * RET_HEADS
    return pl.pallas_call(
        _ret_kernel,
        grid=(batch, nc),
        in_specs=[pl.BlockSpec((slots, c), lambda b, t: (0, 0)),
                  qk(fwd), qk(fwd), vv(fwd), qk(bwd), qk(bwd), vv(bwd)],
        out_specs=[vv(fwd), vv(bwd)],
        out_shape=[jax.ShapeDtypeStruct((n, RET_V_W), _F32)] * 2,
        scratch_shapes=[pltpu.VMEM((slots, c, c), _F32),
                        pltpu.VMEM((slots, c, LANES), _F32),
                        pltpu.VMEM((slots, c, LANES), _F32),
                        pltpu.VMEM((slots, 8, LANES), _F32),
                        pltpu.VMEM((slots, RET_KEY_DIM, RET_VALUE_DIM), _F32)],
        compiler_params=_params("arbitrary", "arbitrary"),
        name=f"retention_t{seq_len}",
    )(dec, rq, rk, rv, rq, rk, rv)


def _merge_kernel(x_ref, g_ref, ao_ref, rf_ref, rb_ref, rg_ref, rn_ref, wg_ref, bg_ref,
                  wba_ref, wbr_ref, wo_ref, o_ref):
    x = x_ref[...]
    h = _rms_rows(x, g_ref[...]).astype(_BF16)
    gates = jax.nn.sigmoid(jnp.dot(h, wg_ref[...], preferred_element_type=_F32) + bg_ref[...])
    ya = jnp.dot(ao_ref[...], wba_ref[...], preferred_element_type=_F32)
    y = rf_ref[...] + rb_ref[...]
    cols = []
    for j in range(0, RET_V_W, RET_VALUE_DIM):
        yh = y[:, j:j + RET_VALUE_DIM]
        mu = jnp.mean(yh, axis=-1, keepdims=True)
        yc = yh - mu
        var = jnp.mean(yc * yc, axis=-1, keepdims=True)
        cols.append(yc * lax.rsqrt(var + NORM_EPS))
    yn = jnp.concatenate(cols, axis=1) * rn_ref[...]
    yr_in = (jax.nn.silu(rg_ref[...]) * yn).astype(_BF16)
    yr = jnp.dot(yr_in, wbr_ref[...], preferred_element_type=_F32)
    mixed = (gates[:, :D_MODEL] * ya + gates[:, D_MODEL:] * yr).astype(_BF16)
    o_ref[...] = x + jnp.dot(mixed, wo_ref[...], preferred_element_type=_F32)


def _merge(x, gain, ao, rf, rb, rg, ret_gain, w_gate, b_gate, w_ba, w_br, w_out):
    n, d = x.shape
    tm = TOKEN_TILE
    row = lambda w: pl.BlockSpec((tm, w), lambda i: (i, 0))
    vec = lambda w: pl.BlockSpec((1, w), lambda i: (0, 0))
    return pl.pallas_call(
        _merge_kernel,
        grid=(n // tm,),
        in_specs=[row(d), vec(d), row(ATTN_Q_W), row(RET_V_W), row(RET_V_W), row(RET_V_W),
                  vec(RET_V_W), _vmem_spec(), vec(2 * d), _vmem_spec(), _vmem_spec(), _vmem_spec()],
        out_specs=row(d),
        out_shape=jax.ShapeDtypeStruct((n, d), _F32),
        compiler_params=_params("parallel"),
        name="merge",
    )(x, gain, ao, rf, rb, rg, ret_gain, w_gate, b_gate, w_ba, w_br, w_out)


def _trunk(xs, p):
    depth = p["ffn1_w13"].shape[0]
    shapes = [x.shape[:2] for x in xs]
    xs = [x.reshape(-1, D_MODEL) for x in xs]
    cos, sin = _rope_tables(max(t for _, t in shapes))

    seg_id = jnp.arange(MXU_TILE) // HEAD_DIM
    seg = (seg_id[:, None] == seg_id[None, :]).astype(_BF16)
    bf = lambda w: w.astype(_BF16)
    vec = lambda g: g.astype(_F32)[None, :]
    final_gain = vec(p["final_norm"])

    for l in range(depth):
        ffn1 = (vec(p["ffn1_norm"][l]), bf(p["ffn1_w13"][l]), bf(p["ffn1_w2"][l]), final_gain)
        ffn2 = (vec(p["ffn2_norm"][l]), bf(p["ffn2_w13"][l]), bf(p["ffn2_w2"][l]), final_gain)
        mix_gain = vec(p["mix_norm"][l])
        proj = (mix_gain, bf(p["w_in"][l]), vec(jnp.tile(p["q_norm"][l], ATTN_HEADS)),
                vec(jnp.tile(p["k_norm"][l], ATTN_KV_HEADS)), seg, cos, sin)
        dec = jnp.concatenate([p["ret_decay_fwd"][l], p["ret_decay_bwd"][l]]).astype(_F32)
        dec = jnp.broadcast_to(dec[:, None], (2 * RET_HEADS, RET_CHUNK))
        merge = (vec(p["ret_norm"][l]), bf(p["w_gate"][l]), vec(p["b_gate"][l]),
                 bf(p["w_branch_attn"][l]), bf(p["w_branch_ret"][l]), bf(p["w_out"][l]))
        for i, (b, t) in enumerate(shapes):
            x = _ffn(xs[i], *ffn1, False)
            q, k, v, rq, rk, rv, rg = _inproj(x, *proj, t)
            ao = _attention(q, k, v, b, t)
            rf, rb = _retention(rq, rk, rv, dec, b, t)
            x = _merge(x, mix_gain, ao, rf, rb, rg, *merge)
            xs[i] = _ffn(x, *ffn2, l == depth - 1)

    return [x.reshape(b, t, D_MODEL) for x, (b, t) in zip(xs, shapes)]


def kernel(x_prompt, x_sample, ffn1_norm, ffn1_w13, ffn1_w2, mix_norm, w_in, q_norm, k_norm,
           ret_decay_fwd, ret_decay_bwd, ret_norm, w_branch_attn, w_branch_ret, w_gate, b_gate,
           w_out, ffn2_norm, ffn2_w13, ffn2_w2, final_norm):
    p = dict(ffn1_norm=ffn1_norm, ffn1_w13=ffn1_w13, ffn1_w2=ffn1_w2, mix_norm=mix_norm, w_in=w_in,
             q_norm=q_norm, k_norm=k_norm, ret_decay_fwd=ret_decay_fwd, ret_decay_bwd=ret_decay_bwd,
             ret_norm=ret_norm, w_branch_attn=w_branch_attn, w_branch_ret=w_branch_ret, w_gate=w_gate,
             b_gate=b_gate, w_out=w_out, ffn2_norm=ffn2_norm, ffn2_w13=ffn2_w13, ffn2_w2=ffn2_w2,
             final_norm=final_norm)
    y_prompt, y_sample = _trunk([x_prompt, x_sample], p)
    return (y_prompt, y_sample)
```

```python
import functools
import math

import jax
import jax.numpy as jnp
from jax import lax
from jax.experimental import pallas as pl
from jax.experimental.pallas import tpu as pltpu

D_MODEL = 1024
GRID_W = 64
HEAD_DIM = 64
ATTN_HEADS = 8
ATTN_KV_HEADS = 2
ATTN_GROUP = ATTN_HEADS // ATTN_KV_HEADS
RET_HEADS = 4
RET_KEY_DIM = HEAD_DIM
RET_VALUE_DIM = 2 * RET_KEY_DIM
D_FF = 2816
ROPE_THETA = 10000.0
ROPE_AXIS_PAIRS = HEAD_DIM // 4
NORM_EPS = 1e-6
ATTN_Q_W = ATTN_HEADS * HEAD_DIM
ATTN_KV_W = ATTN_KV_HEADS * HEAD_DIM
RET_QK_W = RET_HEADS * RET_KEY_DIM
RET_V_W = RET_HEADS * RET_VALUE_DIM
IN_PROJ_W = ATTN_Q_W + 2 * ATTN_KV_W + 2 * RET_QK_W + 2 * RET_V_W

LANES = 128
MXU_TILE = 256
VMEM_LIMIT = 56 * 1024 * 1024

TOKEN_TILE = 512
FF_CHUNK = MXU_TILE
ATTN_TQ = 256
ATTN_TK = 512
VT_ROWS = HEAD_DIM + 16
ATTN_UNROLL = 8
RET_CHUNK = 128
RET_BLOCK = 512
ROPE_ROWS = 1024

_BF16 = jnp.bfloat16
_F32 = jnp.float32
_NT = (((1,), (1,)), ((), ()))
_TN = (((0,), (0,)), ((), ()))


def _vmem_spec():
    return pl.BlockSpec(memory_space=pltpu.VMEM)


def _params(*sem):
    return pltpu.CompilerParams(dimension_semantics=sem, vmem_limit_bytes=VMEM_LIMIT)


def _rms_rows(x, gain):
    ms = jnp.mean(x * x, axis=-1, keepdims=True)
    return x * lax.rsqrt(ms + NORM_EPS) * gain


def _rope_table_kernel(freq_ref, cos_ref, sin_ref):
    rows = cos_ref.shape[0]
    t = pl.program_id(0) * rows + lax.broadcasted_iota(jnp.int32, (rows, LANES), 0)
    lane = lax.broadcasted_iota(jnp.int32, (rows, LANES), 1)
    pair = lane & (2 * ROPE_AXIS_PAIRS - 1)
    shift = GRID_W.bit_length() - 1
    pos = jnp.where(pair < ROPE_AXIS_PAIRS, t >> shift, t & (GRID_W - 1)).astype(_F32)
    ang = pos * freq_ref[...]
    sign = jnp.where((lane & (HEAD_DIM - 1)) < HEAD_DIM // 2, -1.0, 1.0).astype(_F32)
    cos_ref[...] = jnp.cos(ang)
    sin_ref[...] = jnp.sin(ang) * sign


def _rope_tables(t_max):
    assert GRID_W & (GRID_W - 1) == 0 and t_max % ROPE_ROWS == 0
    freqs = ROPE_THETA ** (-jnp.arange(ROPE_AXIS_PAIRS, dtype=_F32) / ROPE_AXIS_PAIRS)
    freq_lanes = jnp.tile(freqs, LANES // ROPE_AXIS_PAIRS)[None, :]
    return pl.pallas_call(
        _rope_table_kernel,
        grid=(t_max // ROPE_ROWS,),
        in_specs=[pl.BlockSpec((1, LANES), lambda i: (0, 0))],
        out_specs=[pl.BlockSpec((ROPE_ROWS, LANES), lambda i: (i, 0))] * 2,
        out_shape=[jax.ShapeDtypeStruct((t_max, LANES), _F32)] * 2,
        compiler_params=_params("parallel"),
        name="rope_tables",
    )(freq_lanes)


def _ffn_kernel(x_ref, g_ref, w13_ref, w2_ref, fg_ref, o_ref, acc_ref, *, final_norm):
    x = x_ref[...]
    h = _rms_rows(x, g_ref[...]).astype(_BF16)
    for c in range(D_FF // FF_CHUNK):
        lo = c * FF_CHUNK
        a = jnp.dot(h, w13_ref[:, lo:lo + FF_CHUNK], preferred_element_type=_F32)
        b = jnp.dot(h, w13_ref[:, D_FF + lo:D_FF + lo + FF_CHUNK], preferred_element_type=_F32)
        act = (jax.nn.silu(a) * b).astype(_BF16)
        part = jnp.dot(act, w2_ref[lo:lo + FF_CHUNK, :], preferred_element_type=_F32)
        if c == 0:
            acc_ref[...] = part
        else:
            acc_ref[...] += part
    y = x + 0.5 * acc_ref[...]
    if final_norm:
        y = _rms_rows(y, fg_ref[...])
    o_ref[...] = y


def _ffn(x, gain, w13, w2, final_gain, final_norm):
    n, d = x.shape
    assert n % TOKEN_TILE == 0 and D_FF % FF_CHUNK == 0
    row = pl.BlockSpec((TOKEN_TILE, d), lambda i: (i, 0))
    vec = pl.BlockSpec((1, d), lambda i: (0, 0))
    return pl.pallas_call(
        functools.partial(_ffn_kernel, final_norm=final_norm),
        grid=(n // TOKEN_TILE,),
        in_specs=[row, vec, _vmem_spec(), _vmem_spec(), vec],
        out_specs=row,
        out_shape=jax.ShapeDtypeStruct((n, d), _F32),
        scratch_shapes=[pltpu.VMEM((TOKEN_TILE, d), _F32)],
        compiler_params=_params("parallel"),
        name="ffn",
    )(x, gain, w13, w2, final_gain)


def _segment_sum(sq, seg):
    hi = sq.astype(_BF16)
    lo = (sq - hi.astype(_F32)).astype(_BF16)
    return (jnp.dot(hi, seg, preferred_element_type=_F32)
            + jnp.dot(lo, seg, preferred_element_type=_F32))


def _rope_lanes(x, cos, sin, first_half):
    partner = jnp.where(first_half, pltpu.roll(x, LANES - HEAD_DIM // 2, axis=1),
                        pltpu.roll(x, HEAD_DIM // 2, axis=1))
    return x * cos + partner * sin


def _inproj_kernel(x_ref, g_ref, w_ref, qg_ref, kg_ref, seg_ref, cos_ref, sin_ref,
                   q_ref, k_ref, v_ref, rq_ref, rk_ref, rv_ref, rg_ref):
    h = _rms_rows(x_ref[...], g_ref[...]).astype(_BF16)
    cos = cos_ref[...]
    sin = sin_ref[...]
    lane = lax.broadcasted_iota(jnp.int32, cos.shape, 1)
    first_half = (lane & (HEAD_DIM - 1)) < HEAD_DIM // 2
    seg = seg_ref[...]
    inv_hd = 1.0 / HEAD_DIM

    def head_norm_rope(xcols, gain, seg_block):
        ss = _segment_sum(xcols * xcols, seg_block)
        xn = xcols * lax.rsqrt(ss * inv_hd + NORM_EPS) * gain
        return [_rope_lanes(xn[:, j:j + LANES], cos, sin, first_half)
                for j in range(0, xn.shape[1], LANES)]

    offset = [0]

    def project(width):
        lo = offset[0]
        offset[0] += width
        return jnp.dot(h, w_ref[:, lo:lo + width], preferred_element_type=_F32)

    aq = project(ATTN_Q_W)
    akv = project(2 * ATTN_KV_W)
    ak, av = akv[:, :ATTN_KV_W], akv[:, ATTN_KV_W:]
    rq = project(RET_QK_W)
    rk = project(RET_QK_W)
    rv = project(RET_V_W)
    rg = project(RET_V_W)

    scale = HEAD_DIM ** -0.5 * math.log2(math.e)
    heads_per_block = LANES // HEAD_DIM
    qcols = []
    for j in range(0, ATTN_Q_W, MXU_TILE):
        qcols += head_norm_rope(aq[:, j:j + MXU_TILE], qg_ref[:, j:j + MXU_TILE], seg)
    for j, col in enumerate(qcols):
        col_t = (col * scale).T.astype(_BF16)
        for s in range(heads_per_block):
            q_ref[heads_per_block * j + s] = col_t[s * HEAD_DIM:(s + 1) * HEAD_DIM]
    kcols = head_norm_rope(ak, kg_ref[...], seg[:ATTN_KV_W, :ATTN_KV_W])
    for j, col in enumerate(kcols):
        col = col.astype(_BF16)
        for s in range(heads_per_block):
            k_ref[heads_per_block * j + s] = col[:, s * HEAD_DIM:(s + 1) * HEAD_DIM]
    av_t = av.T.astype(_BF16)
    ones = jnp.ones((VT_ROWS - HEAD_DIM, av_t.shape[1]), _BF16)
    for g in range(ATTN_KV_HEADS):
        v_ref[g, 0, :HEAD_DIM] = av_t[g * HEAD_DIM:(g + 1) * HEAD_DIM]
        v_ref[g, 0, HEAD_DIM:] = ones

    for j in range(0, RET_QK_W, LANES):
        rq_ref[:, j:j + LANES] = _rope_lanes(rq[:, j:j + LANES], cos, sin, first_half)
        rk_ref[:, j:j + LANES] = (_rope_lanes(rk[:, j:j + LANES], cos, sin, first_half)
                                  * (RET_KEY_DIM ** -0.5))
    rv_ref[...] = rv.astype(_BF16)
    rg_ref[...] = rg


def _inproj(x, gain, w_in, q_gain, k_gain, seg, cos, sin, seq_len):
    n, d = x.shape
    tm = TOKEN_TILE
    assert n % tm == 0 and seq_len % tm == 0 and tm == ATTN_TK
    row = lambda w: pl.BlockSpec((tm, w), lambda i: (i, 0))
    vec = lambda w: pl.BlockSpec((1, w), lambda i: (0, 0))
    heads = lambda nh: pl.BlockSpec((nh, tm, HEAD_DIM), lambda i: (0, i, 0))
    heads_t = pl.BlockSpec((ATTN_HEADS, HEAD_DIM, tm), lambda i: (0, 0, i))
    chunks_t = pl.BlockSpec((ATTN_KV_HEADS, 1, VT_ROWS, tm), lambda i: (0, i, 0, 0))
    table = pl.BlockSpec((tm, LANES), lambda i: (i % (seq_len // tm), 0))
    return pl.pallas_call(
        _inproj_kernel,
        grid=(n // tm,),
        in_specs=[row(d), vec(d), _vmem_spec(), vec(ATTN_Q_W), vec(ATTN_KV_W), _vmem_spec(),
                  table, table],
        out_specs=[heads_t, heads(ATTN_KV_HEADS), chunks_t,
                   row(RET_QK_W), row(RET_QK_W), row(RET_V_W), row(RET_V_W)],
        out_shape=[jax.ShapeDtypeStruct((ATTN_HEADS, HEAD_DIM, n), _BF16),
                   jax.ShapeDtypeStruct((ATTN_KV_HEADS, n, HEAD_DIM), _BF16),
                   jax.ShapeDtypeStruct((ATTN_KV_HEADS, n // tm, VT_ROWS, tm), _BF16),
                   jax.ShapeDtypeStruct((n, RET_QK_W), _F32),
                   jax.ShapeDtypeStruct((n, RET_QK_W), _F32),
                   jax.ShapeDtypeStruct((n, RET_V_W), _BF16),
                   jax.ShapeDtypeStruct((n, RET_V_W), _F32)],
        compiler_params=_params("parallel"),
        name="inproj",
    )(x, gain, w_in, q_gain, k_gain, seg, cos, sin)


def _attn_kernel(qt_ref, k_ref, vt_ref, o_ref, acc_ref, s0_ref, s1_ref, s2_ref, s3_ref, *, seq_len):
    qt = jnp.concatenate([qt_ref[h] for h in range(ATTN_GROUP)], axis=1)
    nk = seq_len // ATTN_TK
    acc_ref[...] = jnp.zeros(acc_ref.shape, _F32)

    def scores(j, dst_ref, m_prev):
        start = pl.multiple_of(j * ATTN_TK, ATTN_TK)
        kj = k_ref[0, pl.ds(start, ATTN_TK), :]
        st = jnp.dot(kj, qt, preferred_element_type=_F32)
        dst_ref[...] = st
        m_next = jnp.maximum(m_prev, jnp.max(st, axis=0, keepdims=True))
        return m_next, jnp.exp2(m_prev - m_next)

    def consume(j, src_ref, m, alpha):
        pt = jnp.exp2(src_ref[...] - m)
        pv = jnp.dot(vt_ref[0, j], pt.astype(_BF16), preferred_element_type=_F32)
        acc_ref[...] = alpha * acc_ref[...] + pv

    bufs = (s0_ref, s1_ref, s2_ref, s3_ref)
    m0, a0 = scores(0, bufs[0], jnp.full((1, qt.shape[1]), -jnp.inf, _F32))
    m1, a1 = scores(1, bufs[1], m0)

    def half_step(j, carry, cur, nxt, lookahead):
        m_a, alpha_a, m_b, alpha_b = carry
        if lookahead:
            m_c, alpha_c = scores(j + 2, nxt[0], m_b)
        consume(j, cur[0], m_a, alpha_a)
        if lookahead:
            m_d, alpha_d = scores(j + 3, nxt[1], m_c)
        consume(j + 1, cur[1], m_b, alpha_b)
        return (m_c, alpha_c, m_d, alpha_d) if lookahead else None

    unroll = ATTN_UNROLL if nk - 4 >= 2 * ATTN_UNROLL else 4

    def body(i, carry):
        j = unroll * i
        for h in range(0, unroll, 4):
            carry = half_step(j + h, carry, bufs[:2], bufs[2:], True)
            carry = half_step(j + h + 2, carry, bufs[2:], bufs[:2], True)
        return carry

    n_body = (nk - 4) // unroll
    carry = lax.fori_loop(0, n_body, body, (m0, a0, m1, a1))
    for j in range(n_body * unroll, nk - 4, 4):
        carry = half_step(j, carry, bufs[:2], bufs[2:], True)
        carry = half_step(j + 2, carry, bufs[2:], bufs[:2], True)
    carry = half_step(nk - 4, carry, bufs[:2], bufs[2:], True)
    half_step(nk - 2, carry, bufs[2:], bufs[:2], False)
    out_t = acc_ref[:HEAD_DIM, :] / acc_ref[HEAD_DIM:HEAD_DIM + 1, :]
    pair = LANES // HEAD_DIM
    for c in range(ATTN_GROUP // pair):
        stacked = jnp.concatenate(
            [out_t[:, (pair * c + s) * ATTN_TQ:(pair * c + s + 1) * ATTN_TQ] for s in range(pair)],
            axis=0)
        o_ref[:, c * LANES:(c + 1) * LANES] = stacked.T.astype(o_ref.dtype)


def _attention(qt, k, vt, batch, seq_len):
    n = k.shape[1]
    assert n == batch * seq_len and seq_len % (4 * ATTN_TK) == 0 and seq_len % ATTN_TQ == 0
    nq = seq_len // ATTN_TQ
    nk = seq_len // ATTN_TK
    cols = ATTN_GROUP * ATTN_TQ
    return pl.pallas_call(
        functools.partial(_attn_kernel, seq_len=seq_len),
        grid=(batch, ATTN_KV_HEADS, nq),
        in_specs=[
            pl.BlockSpec((ATTN_GROUP, HEAD_DIM, ATTN_TQ), lambda b, g, i: (g, 0, b * nq + i)),
            pl.BlockSpec((1, seq_len, HEAD_DIM), lambda b, g, i: (g, b, 0)),
            pl.BlockSpec((1, nk, VT_ROWS, ATTN_TK), lambda b, g, i: (g, b, 0, 0)),
        ],
        out_specs=pl.BlockSpec((ATTN_TQ, ATTN_GROUP * HEAD_DIM), lambda b, g, i: (b * nq + i, g)),
        out_shape=jax.ShapeDtypeStruct((n, ATTN_Q_W), _BF16),
        scratch_shapes=[pltpu.VMEM((VT_ROWS, cols), _F32),
                        ] + [pltpu.VMEM((ATTN_TK, cols), _F32)] * 4,
        compiler_params=_params("parallel", "parallel", "arbitrary"),
        name=f"attn_t{seq_len}",
    )(qt, k, vt)


def _ret_kernel(dec_ref, qf_ref, kf_ref, vf_ref, qb_ref, kb_ref, vb_ref, of_ref, ob_ref,
                dmat_ref, qd_ref, kd_ref, cd_ref, s_ref):
    c = RET_CHUNK
    t = pl.program_id(1)

    @pl.when(t == 0)
    def _init():
        row = lax.broadcasted_iota(jnp.int32, (c, c), 0).astype(_F32)
        col = lax.broadcasted_iota(jnp.int32, (c, c), 1).astype(_F32)
        idx = lax.broadcasted_iota(jnp.int32, (c, LANES), 0).astype(_F32)
        for slot in range(2 * RET_HEADS):
            fwd = slot < RET_HEADS
            lg_c = jnp.log1p(-jnp.exp(dec_ref[slot:slot + 1, :]))
            lg = lg_c[:, :LANES]
            diff = (row - col) if fwd else (col - row)
            mask = (diff >= 0) if fwd else (diff > 0)
            dmat_ref[slot] = jnp.where(mask, jnp.exp(jnp.where(mask, diff, 0.0) * lg_c), 0.0)
            qd_ref[slot] = jnp.exp(((idx + 1.0) if fwd else (c - idx)) * lg)
            kd_ref[slot] = jnp.exp(((c - 1.0 - idx) if fwd else idx) * lg)
            cd_ref[slot] = jnp.exp(c * lg) * jnp.ones((8, LANES), _F32)
        s_ref[...] = jnp.zeros(s_ref.shape, _F32)

    slots = range(2 * RET_HEADS)
    n_chunks = qf_ref.shape[0] // c
    items = [(ci, s) for ci in range(n_chunks) for s in slots]
    qs, ks, vs, rows = {}, {}, {}, {}
    for ci, s in items:
        fwd = s < RET_HEADS
        hd = s % RET_HEADS
        r0 = (ci if fwd else n_chunks - 1 - ci) * c
        key_cols = slice(hd * RET_KEY_DIM, (hd + 1) * RET_KEY_DIM)
        qs[ci, s] = (qf_ref if fwd else qb_ref)[r0:r0 + c, key_cols]
        ks[ci, s] = (kf_ref if fwd else kb_ref)[r0:r0 + c, key_cols]
        vs[ci, s] = (vf_ref if fwd else vb_ref)[r0:r0 + c,
                                                hd * RET_VALUE_DIM:(hd + 1) * RET_VALUE_DIM]
        rows[ci, s] = r0
    qk = {i: lax.dot_general(qs[i].astype(_BF16), ks[i].astype(_BF16), _NT,
                             preferred_element_type=_F32) for i in items}
    kv = {i: lax.dot_general((ks[i] * kd_ref[i[1]][:, :RET_KEY_DIM]).astype(_BF16), vs[i], _TN,
                             preferred_element_type=_F32) for i in items}
    states = {}
    for s in slots:
        state = s_ref[s]
        for ci in range(n_chunks):
            states[ci, s] = state
            state = state * cd_ref[s][:1, :] + kv[ci, s]
        s_ref[s] = state
    for i in items:
        s = i[1]
        lhs = jnp.concatenate([(qk[i] * dmat_ref[s]).astype(_BF16),
                               (qs[i] * qd_ref[s][:, :RET_KEY_DIM]).astype(_BF16)], axis=1)
        rhs = jnp.concatenate([vs[i], states[i].astype(_BF16)], axis=0)
        out_ref = of_ref if s < RET_HEADS else ob_ref
        hd = s % RET_HEADS
        out_ref[rows[i]:rows[i] + c, hd * RET_VALUE_DIM:(hd + 1) * RET_VALUE_DIM] = jnp.dot(
            lhs, rhs, preferred_element_type=_F32)


def _retention(rq, rk, rv, dec, batch, seq_len):
    n = rq.shape[0]
    c = RET_CHUNK
    blk = RET_BLOCK
    assert n == batch * seq_len and seq_len % blk == 0 and blk % c == 0 and RET_VALUE_DIM == LANES
    nc = seq_len // blk
    fwd = lambda b, t: b * nc + t
    bwd = lambda b, t: b * nc + (nc - 1 - t)
    qk = lambda f: pl.BlockSpec((blk, RET_QK_W), lambda b, t: (f(b, t), 0))
    vv = lambda f: pl.BlockSpec((blk, RET_V_W), lambda b, t: (f(b, t), 0))
    slots = 2 * RET_HEADS
    return pl.pallas_call(
        _ret_kernel,
        grid=(batch, nc),
        in_specs=[pl.BlockSpec((slots, c), lambda b, t: (0, 0)),
                  qk(fwd), qk(fwd), vv(fwd), qk(bwd), qk(bwd), vv(bwd)],
        out_specs=[vv(fwd), vv(bwd)],
        out_shape=[jax.ShapeDtypeStruct((n, RET_V_W), _F32)] * 2,
        scratch_shapes=[pltpu.VMEM((slots, c, c), _F32),
                        pltpu.VMEM((slots, c, LANES), _F32),
                        pltpu.VMEM((slots, c, LANES), _F32),
                        pltpu.VMEM((slots, 8, LANES), _F32),
                        pltpu.VMEM((slots, RET_KEY_DIM, RET_VALUE_DIM), _F32)],
        compiler_params=_params("arbitrary", "arbitrary"),
        name=f"retention_t{seq_len}",
    )(dec, rq, rk, rv, rq, rk, rv)


def _merge_kernel(x_ref, g_ref, ao_ref, rf_ref, rb_ref, rg_ref, rn_ref, wg_ref, bg_ref,
                  wba_ref, wbr_ref, wo_ref, o_ref):
    x = x_ref[...]
    h = _rms_rows(x, g_ref[...]).astype(_BF16)
    gates = jax.nn.sigmoid(jnp.dot(h, wg_ref[...], preferred_element_type=_F32) + bg_ref[...])
    ya = jnp.dot(ao_ref[...], wba_ref[...], preferred_element_type=_F32)
    y = rf_ref[...] + rb_ref[...]
    cols = []
    for j in range(0, RET_V_W, RET_VALUE_DIM):
        yh = y[:, j:j + RET_VALUE_DIM]
        mu = jnp.mean(yh, axis=-1, keepdims=True)
        yc = yh - mu
        var = jnp.mean(yc * yc, axis=-1, keepdims=True)
        cols.append(yc * lax.rsqrt(var + NORM_EPS))
    yn = jnp.concatenate(cols, axis=1) * rn_ref[...]
    yr_in = (jax.nn.silu(rg_ref[...]) * yn).astype(_BF16)
    yr = jnp.dot(yr_in, wbr_ref[...], preferred_element_type=_F32)
    mixed = (gates[:, :D_MODEL] * ya + gates[:, D_MODEL:] * yr).astype(_BF16)
    o_ref[...] = x + jnp.dot(mixed, wo_ref[...], preferred_element_type=_F32)


def _merge(x, gain, ao, rf, rb, rg, ret_gain, w_gate, b_gate, w_ba, w_br, w_out):
    n, d = x.shape
    tm = TOKEN_TILE
    row = lambda w: pl.BlockSpec((tm, w), lambda i: (i, 0))
    vec = lambda w: pl.BlockSpec((1, w), lambda i: (0, 0))
    return pl.pallas_call(
        _merge_kernel,
        grid=(n // tm,),
        in_specs=[row(d), vec(d), row(ATTN_Q_W), row(RET_V_W), row(RET_V_W), row(RET_V_W),
                  vec(RET_V_W), _vmem_spec(), vec(2 * d), _vmem_spec(), _vmem_spec(), _vmem_spec()],
        out_specs=row(d),
        out_shape=jax.ShapeDtypeStruct((n, d), _F32),
        compiler_params=_params("parallel"),
        name="merge",
    )(x, gain, ao, rf, rb, rg, ret_gain, w_gate, b_gate, w_ba, w_br, w_out)


def _trunk(xs, p):
    depth = p["ffn1_w13"].shape[0]
    shapes = [x.shape[:2] for x in xs]
    xs = [x.reshape(-1, D_MODEL) for x in xs]
    cos, sin = _rope_tables(max(t for _, t in shapes))

    seg_id = jnp.arange(MXU_TILE) // HEAD_DIM
    seg = (seg_id[:, None] == seg_id[None, :]).astype(_BF16)
    bf = lambda w: w.astype(_BF16)
    vec = lambda g: g.astype(_F32)[None, :]
    final_gain = vec(p["final_norm"])

    for l in range(depth):
        ffn1 = (vec(p["ffn1_norm"][l]), bf(p["ffn1_w13"][l]), bf(p["ffn1_w2"][l]), final_gain)
        ffn2 = (vec(p["ffn2_norm"][l]), bf(p["ffn2_w13"][l]), bf(p["ffn2_w2"][l]), final_gain)
        mix_gain = vec(p["mix_norm"][l])
        proj = (mix_gain, bf(p["w_in"][l]), vec(jnp.tile(p["q_norm"][l], ATTN_HEADS)),
                vec(jnp.tile(p["k_norm"][l], ATTN_KV_HEADS)), seg, cos, sin)
        dec = jnp.concatenate([p["ret_decay_fwd"][l], p["ret_decay_bwd"][l]]).astype(_F32)
        dec = jnp.broadcast_to(dec[:, None], (2 * RET_HEADS, RET_CHUNK))
        merge = (vec(p["ret_norm"][l]), bf(p["w_gate"][l]), vec(p["b_gate"][l]),
                 bf(p["w_branch_attn"][l]), bf(p["w_branch_ret"][l]), bf(p["w_out"][l]))
        for i, (b, t) in enumerate(shapes):
            x = _ffn(xs[i], *ffn1, False)
            q, k, v, rq, rk, rv, rg = _inproj(x, *proj, t)
            ao = _attention(q, k, v, b, t)
            rf, rb = _retention(rq, rk, rv, dec, b, t)
            x = _merge(x, mix_gain, ao, rf, rb, rg, *merge)
            xs[i] = _ffn(x, *ffn2, l == depth - 1)

    return [x.reshape(b, t, D_MODEL) for x, (b, t) in zip(xs, shapes)]


def kernel(x_prompt, x_sample, ffn1_norm, ffn1_w13, ffn1_w2, mix_norm, w_in, q_norm, k_norm,
           ret_decay_fwd, ret_decay_bwd, ret_norm, w_branch_attn, w_branch_ret, w_gate, b_gate,
           w_out, ffn2_norm, ffn2_w13, ffn2_w2, final_norm):
    p = dict(ffn1_norm=ffn1_norm, ffn1_w13=ffn1_w13, ffn1_w2=ffn1_w2, mix_norm=mix_norm, w_in=w_in,
             q_norm=q_norm, k_norm=k_norm, ret_decay_fwd=ret_decay_fwd, ret_decay_bwd=ret_decay_bwd,
             ret_norm=ret_norm, w_branch_attn=w_branch_attn, w_branch_ret=w_branch_ret, w_gate=w_gate,
             b_gate=b_gate, w_out=w_out, ffn2_norm=ffn2_norm, ffn2_w13=ffn2_w13, ffn2_w2=ffn2_w2,
             final_norm=final_norm)
    y_prompt, y_sample = _trunk([x_prompt, x_sample], p)
    return (y_prompt, y_sample)
```

```python
import functools
import math

import jax
import jax.numpy as jnp
from jax import lax
from jax.experimental import pallas as pl
from jax.experimental.pallas import tpu as pltpu

D_MODEL = 1024
GRID_W = 64
HEAD_DIM = 64
ATTN_HEADS = 8
ATTN_KV_HEADS = 2
ATTN_GROUP = ATTN_HEADS // ATTN_KV_HEADS
RET_HEADS = 4
RET_KEY_DIM = HEAD_DIM
RET_VALUE_DIM = 2 * RET_KEY_DIM
D_FF = 2816
ROPE_THETA = 10000.0
ROPE_AXIS_PAIRS = HEAD_DIM // 4
NORM_EPS = 1e-6
ATTN_Q_W = ATTN_HEADS * HEAD_DIM
ATTN_KV_W = ATTN_KV_HEADS * HEAD_DIM
RET_QK_W = RET_HEADS * RET_KEY_DIM
RET_V_W = RET_HEADS * RET_VALUE_DIM
IN_PROJ_W = ATTN_Q_W + 2 * ATTN_KV_W + 2 * RET_QK_W + 2 * RET_V_W

LANES = 128
MXU_TILE = 256
VMEM_LIMIT = 56 * 1024 * 1024

TOKEN_TILE = 512
FF_CHUNK = MXU_TILE
ATTN_TQ = 256
ATTN_TQ_SHORT = 512
ATTN_TK = 512
VT_ROWS = HEAD_DIM + 16
ATTN_UNROLL = 8
RET_CHUNK = 128
RET_BLOCK = 512
ROPE_ROWS = 1024

_BF16 = jnp.bfloat16
_F32 = jnp.float32
_NT = (((1,), (1,)), ((), ()))
_TN = (((0,), (0,)), ((), ()))


def _vmem_spec():
    return pl.BlockSpec(memory_space=pltpu.VMEM)


def _params(*sem):
    return pltpu.CompilerParams(dimension_semantics=sem, vmem_limit_bytes=VMEM_LIMIT)


def _rms_rows(x, gain):
    ms = jnp.mean(x * x, axis=-1, keepdims=True)
    return x * lax.rsqrt(ms + NORM_EPS) * gain


def _rope_table_kernel(freq_ref, cos_ref, sin_ref):
    rows = cos_ref.shape[0]
    t = pl.program_id(0) * rows + lax.broadcasted_iota(jnp.int32, (rows, LANES), 0)
    lane = lax.broadcasted_iota(jnp.int32, (rows, LANES), 1)
    pair = lane & (2 * ROPE_AXIS_PAIRS - 1)
    shift = GRID_W.bit_length() - 1
    pos = jnp.where(pair < ROPE_AXIS_PAIRS, t >> shift, t & (GRID_W - 1)).astype(_F32)
    ang = pos * freq_ref[...]
    sign = jnp.where((lane & (HEAD_DIM - 1)) < HEAD_DIM // 2, -1.0, 1.0).astype(_F32)
    cos_ref[...] = jnp.cos(ang)
    sin_ref[...] = jnp.sin(ang) * sign


def _rope_tables(t_max):
    assert GRID_W & (GRID_W - 1) == 0 and t_max % ROPE_ROWS == 0
    freqs = ROPE_THETA ** (-jnp.arange(ROPE_AXIS_PAIRS, dtype=_F32) / ROPE_AXIS_PAIRS)
    freq_lanes = jnp.tile(freqs, LANES // ROPE_AXIS_PAIRS)[None, :]
    return pl.pallas_call(
        _rope_table_kernel,
        grid=(t_max // ROPE_ROWS,),
        in_specs=[pl.BlockSpec((1, LANES), lambda i: (0, 0))],
        out_specs=[pl.BlockSpec((ROPE_ROWS, LANES), lambda i: (i, 0))] * 2,
        out_shape=[jax.ShapeDtypeStruct((t_max, LANES), _F32)] * 2,
        compiler_params=_params("parallel"),
        name="rope_tables",
    )(freq_lanes)


def _ffn_kernel(x_ref, g_ref, w13_ref, w2_ref, fg_ref, o_ref, acc_ref, *, final_norm):
    x = x_ref[...]
    h = _rms_rows(x, g_ref[...]).astype(_BF16)
    for c in range(D_FF // FF_CHUNK):
        lo = c * FF_CHUNK
        a = jnp.dot(h, w13_ref[:, lo:lo + FF_CHUNK], preferred_element_type=_F32)
        b = jnp.dot(h, w13_ref[:, D_FF + lo:D_FF + lo + FF_CHUNK], preferred_element_type=_F32)
        act = (jax.nn.silu(a) * b).astype(_BF16)
        part = jnp.dot(act, w2_ref[lo:lo + FF_CHUNK, :], preferred_element_type=_F32)
        if c == 0:
            acc_ref[...] = part
        else:
            acc_ref[...] += part
    y = x + 0.5 * acc_ref[...]
    if final_norm:
        y = _rms_rows(y, fg_ref[...])
    o_ref[...] = y


def _ffn(x, gain, w13, w2, final_gain, final_norm):
    n, d = x.shape
    assert n % TOKEN_TILE == 0 and D_FF % FF_CHUNK == 0
    row = pl.BlockSpec((TOKEN_TILE, d), lambda i: (i, 0))
    vec = pl.BlockSpec((1, d), lambda i: (0, 0))
    return pl.pallas_call(
        functools.partial(_ffn_kernel, final_norm=final_norm),
        grid=(n // TOKEN_TILE,),
        in_specs=[row, vec, _vmem_spec(), _vmem_spec(), vec],
        out_specs=row,
        out_shape=jax.ShapeDtypeStruct((n, d), _F32),
        scratch_shapes=[pltpu.VMEM((TOKEN_TILE, d), _F32)],
        compiler_params=_params("parallel"),
        name="ffn",
    )(x, gain, w13, w2, final_gain)


def _segment_sum(sq, seg):
    hi = sq.astype(_BF16)
    lo = (sq - hi.astype(_F32)).astype(_BF16)
    return (jnp.dot(hi, seg, preferred_element_type=_F32)
            + jnp.dot(lo, seg, preferred_element_type=_F32))


def _rope_lanes(x, cos, sin, first_half):
    partner = jnp.where(first_half, pltpu.roll(x, LANES - HEAD_DIM // 2, axis=1),
                        pltpu.roll(x, HEAD_DIM // 2, axis=1))
    return x * cos + partner * sin


def _inproj_kernel(x_ref, g_ref, w_ref, qg_ref, kg_ref, seg_ref, cos_ref, sin_ref,
                   q_ref, k_ref, v_ref, rq_ref, rk_ref, rv_ref, rg_ref):
    h = _rms_rows(x_ref[...], g_ref[...]).astype(_BF16)
    cos = cos_ref[...]
    sin = sin_ref[...]
    lane = lax.broadcasted_iota(jnp.int32, cos.shape, 1)
    first_half = (lane & (HEAD_DIM - 1)) < HEAD_DIM // 2
    seg = seg_ref[...]
    inv_hd = 1.0 / HEAD_DIM

    def head_norm_rope(xcols, gain, seg_block):
        ss = _segment_sum(xcols * xcols, seg_block)
        xn = xcols * lax.rsqrt(ss * inv_hd + NORM_EPS) * gain
        return [_rope_lanes(xn[:, j:j + LANES], cos, sin, first_half)
                for j in range(0, xn.shape[1], LANES)]

    offset = [0]

    def project(width):
        lo = offset[0]
        offset[0] += width
        return jnp.dot(h, w_ref[:, lo:lo + width], preferred_element_type=_F32)

    scale = HEAD_DIM ** -0.5 * math.log2(math.e)
    heads_per_block = LANES // HEAD_DIM
    aq = project(ATTN_Q_W)
    qcols = []
    for j in range(0, ATTN_Q_W, MXU_TILE):
        qcols += head_norm_rope(aq[:, j:j + MXU_TILE], qg_ref[:, j:j + MXU_TILE], seg)
    for j, col in enumerate(qcols):
        col_t = (col * scale).T.astype(_BF16)
        for s in range(heads_per_block):
            q_ref[heads_per_block * j + s] = col_t[s * HEAD_DIM:(s + 1) * HEAD_DIM]
    akv = project(2 * ATTN_KV_W)
    ak, av = akv[:, :ATTN_KV_W], akv[:, ATTN_KV_W:]
    kcols = head_norm_rope(ak, kg_ref[...], seg[:ATTN_KV_W, :ATTN_KV_W])
    for j, col in enumerate(kcols):
        col = col.astype(_BF16)
        for s in range(heads_per_block):
            k_ref[heads_per_block * j + s] = col[:, s * HEAD_DIM:(s + 1) * HEAD_DIM]
    av_t = av.T.astype(_BF16)
    ones = jnp.ones((VT_ROWS - HEAD_DIM, av_t.shape[1]), _BF16)
    for g in range(ATTN_KV_HEADS):
        v_ref[g, 0, :HEAD_DIM] = av_t[g * HEAD_DIM:(g + 1) * HEAD_DIM]
        v_ref[g, 0, HEAD_DIM:] = ones

    rq = project(RET_QK_W)
    for j in range(0, RET_QK_W, LANES):
        rq_ref[:, j:j + LANES] = _rope_lanes(rq[:, j:j + LANES], cos, sin, first_half)
    rk = project(RET_QK_W)
    for j in range(0, RET_QK_W, LANES):
        rk_ref[:, j:j + LANES] = (_rope_lanes(rk[:, j:j + LANES], cos, sin, first_half)
                                  * (RET_KEY_DIM ** -0.5))
    rv_ref[...] = project(RET_V_W).astype(_BF16)
    rg_ref[...] = project(RET_V_W)


def _inproj(x, gain, w_in, q_gain, k_gain, seg, cos, sin, seq_len):
    n, d = x.shape
    tm = TOKEN_TILE
    assert n % tm == 0 and seq_len % tm == 0 and tm == ATTN_TK
    row = lambda w: pl.BlockSpec((tm, w), lambda i: (i, 0))
    vec = lambda w: pl.BlockSpec((1, w), lambda i: (0, 0))
    heads = lambda nh: pl.BlockSpec((nh, tm, HEAD_DIM), lambda i: (0, i, 0))
    heads_t = pl.BlockSpec((ATTN_HEADS, HEAD_DIM, tm), lambda i: (0, 0, i))
    chunks_t = pl.BlockSpec((ATTN_KV_HEADS, 1, VT_ROWS, tm), lambda i: (0, i, 0, 0))
    table = pl.BlockSpec((tm, LANES), lambda i: (i % (seq_len // tm), 0))
    return pl.pallas_call(
        _inproj_kernel,
        grid=(n // tm,),
        in_specs=[row(d), vec(d), _vmem_spec(), vec(ATTN_Q_W), vec(ATTN_KV_W), _vmem_spec(),
                  table, table],
        out_specs=[heads_t, heads(ATTN_KV_HEADS), chunks_t,
                   row(RET_QK_W), row(RET_QK_W), row(RET_V_W), row(RET_V_W)],
        out_shape=[jax.ShapeDtypeStruct((ATTN_HEADS, HEAD_DIM, n), _BF16),
                   jax.ShapeDtypeStruct((ATTN_KV_HEADS, n, HEAD_DIM), _BF16),
                   jax.ShapeDtypeStruct((ATTN_KV_HEADS, n // tm, VT_ROWS, tm), _BF16),
                   jax.ShapeDtypeStruct((n, RET_QK_W), _F32),
                   jax.ShapeDtypeStruct((n, RET_QK_W), _F32),
                   jax.ShapeDtypeStruct((n, RET_V_W), _BF16),
                   jax.ShapeDtypeStruct((n, RET_V_W), _F32)],
        compiler_params=_params("parallel"),
        name="inproj",
    )(x, gain, w_in, q_gain, k_gain, seg, cos, sin)


def _attn_kernel(qt_ref, k_ref, vt_ref, o_ref, acc_ref, s0_ref, s1_ref, s2_ref, s3_ref, *, seq_len):
    qt = jnp.concatenate([qt_ref[h] for h in range(ATTN_GROUP)], axis=1)
    tq = qt_ref.shape[2]
    nk = seq_len // ATTN_TK
    acc_ref[...] = jnp.zeros(acc_ref.shape, _F32)

    def scores(j, dst_ref, m_prev):
        start = pl.multiple_of(j * ATTN_TK, ATTN_TK)
        kj = k_ref[0, pl.ds(start, ATTN_TK), :]
        st = jnp.dot(kj, qt, preferred_element_type=_F32)
        dst_ref[...] = st
        m_next = jnp.maximum(m_prev, jnp.max(st, axis=0, keepdims=True))
        return m_next, jnp.exp2(m_prev - m_next)

    def consume(j, src_ref, m, alpha):
        pt = jnp.exp2(src_ref[...] - m)
        pv = jnp.dot(vt_ref[0, j], pt.astype(_BF16), preferred_element_type=_F32)
        acc_ref[...] = alpha * acc_ref[...] + pv

    bufs = (s0_ref, s1_ref, s2_ref, s3_ref)
    m0, a0 = scores(0, bufs[0], jnp.full((1, qt.shape[1]), -jnp.inf, _F32))
    m1, a1 = scores(1, bufs[1], m0)

    def half_step(j, carry, cur, nxt, lookahead):
        m_a, alpha_a, m_b, alpha_b = carry
        if lookahead:
            m_c, alpha_c = scores(j + 2, nxt[0], m_b)
        consume(j, cur[0], m_a, alpha_a)
        if lookahead:
            m_d, alpha_d = scores(j + 3, nxt[1], m_c)
        consume(j + 1, cur[1], m_b, alpha_b)
        return (m_c, alpha_c, m_d, alpha_d) if lookahead else None

    unroll = ATTN_UNROLL if nk - 4 >= 2 * ATTN_UNROLL else 4

    def body(i, carry):
        j = unroll * i
        for h in range(0, unroll, 4):
            carry = half_step(j + h, carry, bufs[:2], bufs[2:], True)
            carry = half_step(j + h + 2, carry, bufs[2:], bufs[:2], True)
        return carry

    n_body = (nk - 4) // unroll
    carry = lax.fori_loop(0, n_body, body, (m0, a0, m1, a1))
    for j in range(n_body * unroll, nk - 4, 4):
        carry = half_step(j, carry, bufs[:2], bufs[2:], True)
        carry = half_step(j + 2, carry, bufs[2:], bufs[:2], True)
    carry = half_step(nk - 4, carry, bufs[:2], bufs[2:], True)
    half_step(nk - 2, carry, bufs[2:], bufs[:2], False)
    out_t = acc_ref[:HEAD_DIM, :] / acc_ref[HEAD_DIM:HEAD_DIM + 1, :]
    pair = LANES // HEAD_DIM
    for c in range(ATTN_GROUP // pair):
        stacked = jnp.concatenate(
            [out_t[:, (pair * c + s) * tq:(pair * c + s + 1) * tq] for s in range(pair)], axis=0)
        o_ref[:, c * LANES:(c + 1) * LANES] = stacked.T.astype(o_ref.dtype)


def _attention(qt, k, vt, batch, seq_len):
    n = k.shape[1]
    nk = seq_len // ATTN_TK
    tq = ATTN_TQ if nk - 4 >= 2 * ATTN_UNROLL else ATTN_TQ_SHORT
    assert n == batch * seq_len and seq_len % (4 * ATTN_TK) == 0 and seq_len % tq == 0
    nq = seq_len // tq
    cols = ATTN_GROUP * tq
    return pl.pallas_call(
        functools.partial(_attn_kernel, seq_len=seq_len),
        grid=(batch, ATTN_KV_HEADS, nq),
        in_specs=[
            pl.BlockSpec((ATTN_GROUP, HEAD_DIM, tq), lambda b, g, i: (g, 0, b * nq + i)),
            pl.BlockSpec((1, seq_len, HEAD_DIM), lambda b, g, i: (g, b, 0)),
            pl.BlockSpec((1, nk, VT_ROWS, ATTN_TK), lambda b, g, i: (g, b, 0, 0)),
        ],
        out_specs=pl.BlockSpec((tq, ATTN_GROUP * HEAD_DIM), lambda b, g, i: (b * nq + i, g)),
        out_shape=jax.ShapeDtypeStruct((n, ATTN_Q_W), _BF16),
        scratch_shapes=[pltpu.VMEM((VT_ROWS, cols), _F32),
                        ] + [pltpu.VMEM((ATTN_TK, cols), _F32)] * 4,
        compiler_params=_params("parallel", "parallel", "arbitrary"),
        name=f"attn_t{seq_len}",
    )(qt, k, vt)


def _ret_kernel(dec_ref, qf_ref, kf_ref, vf_ref, qb_ref, kb_ref, vb_ref, of_ref, ob_ref,
                dmat_ref, qd_ref, kd_ref, cd_ref, s_ref):
    c = RET_CHUNK
    t = pl.program_id(1)

    @pl.when(t == 0)
    def _init():
        row = lax.broadcasted_iota(jnp.int32, (c, c), 0).astype(_F32)
        col = lax.broadcasted_iota(jnp.int32, (c, c), 1).astype(_F32)
        idx = lax.broadcasted_iota(jnp.int32, (c, LANES), 0).astype(_F32)
        for slot in range(2 * RET_HEADS):
            fwd = slot < RET_HEADS
            lg_c = jnp.log1p(-jnp.exp(dec_ref[slot:slot + 1, :]))
            lg = lg_c[:, :LANES]
            diff = (row - col) if fwd else (col - row)
            mask = (diff >= 0) if fwd else (diff > 0)
            dmat_ref[slot] = jnp.where(mask, jnp.exp(jnp.where(mask, diff, 0.0) * lg_c), 0.0)
            qd_ref[slot] = jnp.exp(((idx + 1.0) if fwd else (c - idx)) * lg)
            kd_ref[slot] = jnp.exp(((c - 1.0 - idx) if fwd else idx) * lg)
            cd_ref[slot] = jnp.exp(c * lg) * jnp.ones((8, LANES), _F32)
        s_ref[...] = jnp.zeros(s_ref.shape, _F32)

    slots = range(2 * RET_HEADS)
    n_chunks = qf_ref.shape[0] // c
    items = [(ci, s) for ci in range(n_chunks) for s in slots]
    qs, ks, vs, rows = {}, {}, {}, {}
    for ci, s in items:
        fwd = s < RET_HEADS
        hd = s % RET_HEADS
        r0 = (ci if fwd else n_chunks - 1 - ci) * c
        key_cols = slice(hd * RET_KEY_DIM, (hd + 1) * RET_KEY_DIM)
        qs[ci, s] = (qf_ref if fwd else qb_ref)[r0:r0 + c, key_cols]
        ks[ci, s] = (kf_ref if fwd else kb_ref)[r0:r0 + c, key_cols]
        vs[ci, s] = (vf_ref if fwd else vb_ref)[r0:r0 + c,
                                                hd * RET_VALUE_DIM:(hd + 1) * RET_VALUE_DIM]
        rows[ci, s] = r0
    qk = {i: lax.dot_general(qs[i].astype(_BF16), ks[i].astype(_BF16), _NT,
                             preferred_element_type=_F32) for i in items}
    kv = {i: lax.dot_general((ks[i] * kd_ref[i[1]][:, :RET_KEY_DIM]).astype(_BF16), vs[i], _TN,
                             preferred_element_type=_F32) for i in items}
    states = {}
    for s in slots:
        state = s_ref[s]
        for ci in range(n_chunks):
            states[ci, s] = state
            state = state * cd_ref[s][:1, :] + kv[ci, s]
        s_ref[s] = state
    for i in items:
        s = i[1]
        lhs = jnp.concatenate([(qk[i] * dmat_ref[s]).astype(_BF16),
                               (qs[i] * qd_ref[s][:, :RET_KEY_DIM]).astype(_BF16)], axis=1)
        rhs = jnp.concatenate([vs[i], states[i].astype(_BF16)], axis=0)
        out_ref = of_ref if s < RET_HEADS else ob_ref
        hd = s % RET_HEADS
        out_ref[rows[i]:rows[i] + c, hd * RET_VALUE_DIM:(hd + 1) * RET_VALUE_DIM] = jnp.dot(
            lhs, rhs, preferred_element_type=_F32)


def _retention(rq, rk, rv, dec, batch, seq_len):
    n = rq.shape[0]
    c = RET_CHUNK
    blk = RET_BLOCK
    assert n == batch * seq_len and seq_len % blk == 0 and blk % c == 0 and RET_VALUE_DIM == LANES
    nc = seq_len // blk
    fwd = lambda b, t: b * nc + t
    bwd = lambda b, t: b * nc + (nc - 1 - t)
    qk = lambda f: pl.BlockSpec((blk, RET_QK_W), lambda b, t: (f(b, t), 0))
    vv = lambda f: pl.BlockSpec((blk, RET_V_W), lambda b, t: (f(b, t), 0))
    slots = 2 * RET_HEADS
    return pl.pallas_call(
        _ret_kernel,
        grid=(batch, nc),
        in_specs=[pl.BlockSpec((slots, c), lambda b, t: (0, 0)),
                  qk(fwd), qk(fwd), vv(fwd), qk(bwd), qk(bwd), vv(bwd)],
        out_specs=[vv(fwd), vv(bwd)],
        out_shape=[jax.ShapeDtypeStruct((n, RET_V_W), _F32)] * 2,
        scratch_shapes=[pltpu.VMEM((slots, c, c), _F32),
                        pltpu.VMEM((slots, c, LANES), _F32),
                        pltpu.VMEM((slots, c, LANES), _F32),
                        pltpu.VMEM((slots, 8, LANES), _F32),
                        pltpu.VMEM((slots, RET_KEY_DIM, RET_VALUE_DIM), _F32)],
        compiler_params=_params("arbitrary", "arbitrary"),
        name=f"retention_t{seq_len}",
    )(dec, rq, rk, rv, rq, rk, rv)


def _merge_kernel(x_ref, g_ref, ao_ref, rf_ref, rb_ref, rg_ref, rn_ref, wg_ref, bg_ref,
                  wba_ref, wbr_ref, wo_ref, o_ref):
    x = x_ref[...]
    h = _rms_rows(x, g_ref[...]).astype(_BF16)
    gates = jax.nn.sigmoid(jnp.dot(h, wg_ref[...], preferred_element_type=_F32) + bg_ref[...])
    ya = jnp.dot(ao_ref[...], wba_ref[...], preferred_element_type=_F32)
    y = rf_ref[...] + rb_ref[...]
    cols = []
    for j in range(0, RET_V_W, RET_VALUE_DIM):
        yh = y[:, j:j + RET_VALUE_DIM]
        mu = jnp.mean(yh, axis=-1, keepdims=True)
        yc = yh - mu
        var = jnp.mean(yc * yc, axis=-1, keepdims=True)
        cols.append(yc * lax.rsqrt(var + NORM_EPS))
    yn = jnp.concatenate(cols, axis=1) * rn_ref[...]
    yr_in = (jax.nn.silu(rg_ref[...]) * yn).astype(_BF16)
    yr = jnp.dot(yr_in, wbr_ref[...], preferred_element_type=_F32)
    mixed = (gates[:, :D_MODEL] * ya + gates[:, D_MODEL:] * yr).astype(_BF16)
    o_ref[...] = x + jnp.dot(mixed, wo_ref[...], preferred_element_type=_F32)


def _merge(x, gain, ao, rf, rb, rg, ret_gain, w_gate, b_gate, w_ba, w_br, w_out):
    n, d = x.shape
    tm = TOKEN_TILE
    row = lambda w: pl.BlockSpec((tm, w), lambda i: (i, 0))
    vec = lambda w: pl.BlockSpec((1, w), lambda i: (0, 0))
    return pl.pallas_call(
        _merge_kernel,
        grid=(n // tm,),
        in_specs=[row(d), vec(d), row(ATTN_Q_W), row(RET_V_W), row(RET_V_W), row(RET_V_W),
                  vec(RET_V_W), _vmem_spec(), vec(2 * d), _vmem_spec(), _vmem_spec(), _vmem_spec()],
        out_specs=row(d),
        out_shape=jax.ShapeDtypeStruct((n, d), _F32),
        compiler_params=_params("parallel"),
        name="merge",
    )(x, gain, ao, rf, rb, rg, ret_gain, w_gate, b_gate, w_ba, w_br, w_out)


def _trunk(xs, p):
    depth = p["ffn1_w13"].shape[0]
    shapes = [x.shape[:2] for x in xs]
    xs = [x.reshape(-1, D_MODEL) for x in xs]
    cos, sin = _rope_tables(max(t for _, t in shapes))

    seg_id = jnp.arange(MXU_TILE) // HEAD_DIM
    seg = (seg_id[:, None] == seg_id[None, :]).astype(_BF16)
    bf = lambda w: w.astype(_BF16)
    vec = lambda g: g.astype(_F32)[None, :]
    final_gain = vec(p["final_norm"])

    for l in range(depth):
        ffn1 = (vec(p["ffn1_norm"][l]), bf(p["ffn1_w13"][l]), bf(p["ffn1_w2"][l]), final_gain)
        ffn2 = (vec(p["ffn2_norm"][l]), bf(p["ffn2_w13"][l]), bf(p["ffn2_w2"][l]), final_gain)
        mix_gain = vec(p["mix_norm"][l])
        proj = (mix_gain, bf(p["w_in"][l]), vec(jnp.tile(p["q_norm"][l], ATTN_HEADS)),
                vec(jnp.tile(p["k_norm"][l], ATTN_KV_HEADS)), seg, cos, sin)
        dec = jnp.concatenate([p["ret_decay_fwd"][l], p["ret_decay_bwd"][l]]).astype(_F32)
        dec = jnp.broadcast_to(dec[:, None], (2 * RET_HEADS, RET_CHUNK))
        merge = (vec(p["ret_norm"][l]), bf(p["w_gate"][l]), vec(p["b_gate"][l]),
                 bf(p["w_branch_attn"][l]), bf(p["w_branch_ret"][l]), bf(p["w_out"][l]))
        for i, (b, t) in enumerate(shapes):
            x = _ffn(xs[i], *ffn1, False)
            q, k, v, rq, rk, rv, rg = _inproj(x, *proj, t)
            ao = _attention(q, k, v, b, t)
            rf, rb = _retention(rq, rk, rv, dec, b, t)
            x = _merge(x, mix_gain, ao, rf, rb, rg, *merge)
            xs[i] = _ffn(x, *ffn2, l == depth - 1)

    return [x.reshape(b, t, D_MODEL) for x, (b, t) in zip(xs, shapes)]


def kernel(x_prompt, x_sample, ffn1_norm, ffn1_w13, ffn1_w2, mix_norm, w_in, q_norm, k_norm,
           ret_decay_fwd, ret_decay_bwd, ret_norm, w_branch_attn, w_branch_ret, w_gate, b_gate,
           w_out, ffn2_norm, ffn2_w13, ffn2_w2, final_norm):
    p = dict(ffn1_norm=ffn1_norm, ffn1_w13=ffn1_w13, ffn1_w2=ffn1_w2, mix_norm=mix_norm, w_in=w_in,
             q_norm=q_norm, k_norm=k_norm, ret_decay_fwd=ret_decay_fwd, ret_decay_bwd=ret_decay_bwd,
             ret_norm=ret_norm, w_branch_attn=w_branch_attn, w_branch_ret=w_branch_ret, w_gate=w_gate,
             b_gate=b_gate, w_out=w_out, ffn2_norm=ffn2_norm, ffn2_w13=ffn2_w13, ffn2_w2=ffn2_w2,
             final_norm=final_norm)
    y_prompt, y_sample = _trunk([x_prompt, x_sample], p)
    return (y_prompt, y_sample)
```

```python
import functools
import math

import jax
import jax.numpy as jnp
from jax import lax
from jax.experimental import pallas as pl
from jax.experimental.pallas import tpu as pltpu

D_MODEL = 1024
GRID_W = 64
HEAD_DIM = 64
ATTN_HEADS = 8
ATTN_KV_HEADS = 2
ATTN_GROUP = ATTN_HEADS // ATTN_KV_HEADS
RET_HEADS = 4
RET_KEY_DIM = HEAD_DIM
RET_VALUE_DIM = 2 * RET_KEY_DIM
D_FF = 2816
ROPE_THETA = 10000.0
ROPE_AXIS_PAIRS = HEAD_DIM // 4
NORM_EPS = 1e-6
ATTN_Q_W = ATTN_HEADS * HEAD_DIM
ATTN_KV_W = ATTN_KV_HEADS * HEAD_DIM
RET_QK_W = RET_HEADS * RET_KEY_DIM
RET_V_W = RET_HEADS * RET_VALUE_DIM
IN_PROJ_W = ATTN_Q_W + 2 * ATTN_KV_W + 2 * RET_QK_W + 2 * RET_V_W

LANES = 128
MXU_TILE = 256
VMEM_LIMIT = 56 * 1024 * 1024

TOKEN_TILE = 512
FF_CHUNK = MXU_TILE
ATTN_TQ = 256
ATTN_TQ_SHORT = 512
ATTN_TK = 512
VT_ROWS = HEAD_DIM + 64
ATTN_UNROLL = 8
ATTN_SPLIT = 2
RET_CHUNK = 128
RET_BLOCK = 512
ROPE_ROWS = 1024

_BF16 = jnp.bfloat16
_F32 = jnp.float32
_NT = (((1,), (1,)), ((), ()))
_TN = (((0,), (0,)), ((), ()))


def _vmem_spec():
    return pl.BlockSpec(memory_space=pltpu.VMEM)


def _params(*sem):
    return pltpu.CompilerParams(dimension_semantics=sem, vmem_limit_bytes=VMEM_LIMIT)


def _rms_rows(x, gain):
    ms = jnp.mean(x * x, axis=-1, keepdims=True)
    return x * lax.rsqrt(ms + NORM_EPS) * gain


def _rope_table_kernel(freq_ref, cos_ref, sin_ref):
    rows = cos_ref.shape[0]
    t = pl.program_id(0) * rows + lax.broadcasted_iota(jnp.int32, (rows, LANES), 0)
    lane = lax.broadcasted_iota(jnp.int32, (rows, LANES), 1)
    pair = lane & (2 * ROPE_AXIS_PAIRS - 1)
    shift = GRID_W.bit_length() - 1
    pos = jnp.where(pair < ROPE_AXIS_PAIRS, t >> shift, t & (GRID_W - 1)).astype(_F32)
    ang = pos * freq_ref[...]
    sign = jnp.where((lane & (HEAD_DIM - 1)) < HEAD_DIM // 2, -1.0, 1.0).astype(_F32)
    cos_ref[...] = jnp.cos(ang)
    sin_ref[...] = jnp.sin(ang) * sign


def _rope_tables(t_max):
    assert GRID_W & (GRID_W - 1) == 0 and t_max % ROPE_ROWS == 0
    freqs = ROPE_THETA ** (-jnp.arange(ROPE_AXIS_PAIRS, dtype=_F32) / ROPE_AXIS_PAIRS)
    freq_lanes = jnp.tile(freqs, LANES // ROPE_AXIS_PAIRS)[None, :]
    return pl.pallas_call(
        _rope_table_kernel,
        grid=(t_max // ROPE_ROWS,),
        in_specs=[pl.BlockSpec((1, LANES), lambda i: (0, 0))],
        out_specs=[pl.BlockSpec((ROPE_ROWS, LANES), lambda i: (i, 0))] * 2,
        out_shape=[jax.ShapeDtypeStruct((t_max, LANES), _F32)] * 2,
        compiler_params=_params("parallel"),
        name="rope_tables",
    )(freq_lanes)


def _ffn_kernel(x_ref, g_ref, w13_ref, w2_ref, fg_ref, o_ref, acc_ref, *, final_norm):
    x = x_ref[...]
    h = _rms_rows(x, g_ref[...]).astype(_BF16)
    for c in range(D_FF // FF_CHUNK):
        lo = c * FF_CHUNK
        a = jnp.dot(h, w13_ref[:, lo:lo + FF_CHUNK], preferred_element_type=_F32)
        b = jnp.dot(h, w13_ref[:, D_FF + lo:D_FF + lo + FF_CHUNK], preferred_element_type=_F32)
        act = (jax.nn.silu(a) * b).astype(_BF16)
        part = jnp.dot(act, w2_ref[lo:lo + FF_CHUNK, :], preferred_element_type=_F32)
        if c == 0:
            acc_ref[...] = part
        else:
            acc_ref[...] += part
    y = x + 0.5 * acc_ref[...]
    if final_norm:
        y = _rms_rows(y, fg_ref[...])
    o_ref[...] = y


def _ffn(x, gain, w13, w2, final_gain, final_norm):
    n, d = x.shape
    assert n % TOKEN_TILE == 0 and D_FF % FF_CHUNK == 0
    row = pl.BlockSpec((TOKEN_TILE, d), lambda i: (i, 0))
    vec = pl.BlockSpec((1, d), lambda i: (0, 0))
    return pl.pallas_call(
        functools.partial(_ffn_kernel, final_norm=final_norm),
        grid=(n // TOKEN_TILE,),
        in_specs=[row, vec, _vmem_spec(), _vmem_spec(), vec],
        out_specs=row,
        out_shape=jax.ShapeDtypeStruct((n, d), _F32),
        scratch_shapes=[pltpu.VMEM((TOKEN_TILE, d), _F32)],
        compiler_params=_params("parallel"),
        name="ffn",
    )(x, gain, w13, w2, final_gain)


def _segment_sum(sq, seg):
    hi = sq.astype(_BF16)
    lo = (sq - hi.astype(_F32)).astype(_BF16)
    return (jnp.dot(hi, seg, preferred_element_type=_F32)
            + jnp.dot(lo, seg, preferred_element_type=_F32))


def _rope_lanes(x, cos, sin, first_half):
    partner = jnp.where(first_half, pltpu.roll(x, LANES - HEAD_DIM // 2, axis=1),
                        pltpu.roll(x, HEAD_DIM // 2, axis=1))
    return x * cos + partner * sin


def _inproj_kernel(x_ref, g_ref, w_ref, qg_ref, kg_ref, seg_ref, cos_ref, sin_ref,
                   q_ref, k_ref, v_ref, rq_ref, rk_ref, rv_ref, rg_ref):
    h = _rms_rows(x_ref[...], g_ref[...]).astype(_BF16)
    cos = cos_ref[...]
    sin = sin_ref[...]
    lane = lax.broadcasted_iota(jnp.int32, cos.shape, 1)
    first_half = (lane & (HEAD_DIM - 1)) < HEAD_DIM // 2
    seg = seg_ref[...]
    inv_hd = 1.0 / HEAD_DIM

    def head_norm_rope(xcols, gain, seg_block):
        ss = _segment_sum(xcols * xcols, seg_block)
        xn = xcols * lax.rsqrt(ss * inv_hd + NORM_EPS) * gain
        return [_rope_lanes(xn[:, j:j + LANES], cos, sin, first_half)
                for j in range(0, xn.shape[1], LANES)]

    offset = [0]

    def project(width):
        lo = offset[0]
        offset[0] += width
        return jnp.dot(h, w_ref[:, lo:lo + width], preferred_element_type=_F32)

    scale = HEAD_DIM ** -0.5 * math.log2(math.e)
    heads_per_block = LANES // HEAD_DIM
    aq = project(ATTN_Q_W)
    qcols = []
    for j in range(0, ATTN_Q_W, MXU_TILE):
        qcols += head_norm_rope(aq[:, j:j + MXU_TILE], qg_ref[:, j:j + MXU_TILE], seg)
    for j, col in enumerate(qcols):
        col_t = (col * scale).T.astype(_BF16)
        for s in range(heads_per_block):
            q_ref[heads_per_block * j + s] = col_t[s * HEAD_DIM:(s + 1) * HEAD_DIM]
    akv = project(2 * ATTN_KV_W)
    ak, av = akv[:, :ATTN_KV_W], akv[:, ATTN_KV_W:]
    kcols = head_norm_rope(ak, kg_ref[...], seg[:ATTN_KV_W, :ATTN_KV_W])
    for j, col in enumerate(kcols):
        col = col.astype(_BF16)
        for s in range(heads_per_block):
            k_ref[heads_per_block * j + s] = col[:, s * HEAD_DIM:(s + 1) * HEAD_DIM]
    av_t = av.T.astype(_BF16)
    ones = jnp.ones((VT_ROWS - HEAD_DIM, av_t.shape[1]), _BF16)
    for g in range(ATTN_KV_HEADS):
        v_ref[g, 0, :HEAD_DIM] = av_t[g * HEAD_DIM:(g + 1) * HEAD_DIM]
        v_ref[g, 0, HEAD_DIM:] = ones

    rq = project(RET_QK_W)
    for j in range(0, RET_QK_W, LANES):
        rq_ref[:, j:j + LANES] = _rope_lanes(rq[:, j:j + LANES], cos, sin, first_half)
    rk = project(RET_QK_W)
    for j in range(0, RET_QK_W, LANES):
        rk_ref[:, j:j + LANES] = (_rope_lanes(rk[:, j:j + LANES], cos, sin, first_half)
                                  * (RET_KEY_DIM ** -0.5))
    rv_ref[...] = project(RET_V_W).astype(_BF16)
    rg_ref[...] = project(RET_V_W)


def _inproj(x, gain, w_in, q_gain, k_gain, seg, cos, sin, seq_len):
    n, d = x.shape
    tm = TOKEN_TILE
    assert n % tm == 0 and seq_len % tm == 0 and tm == ATTN_TK
    row = lambda w: pl.BlockSpec((tm, w), lambda i: (i, 0))
    vec = lambda w: pl.BlockSpec((1, w), lambda i: (0, 0))
    heads = lambda nh: pl.BlockSpec((nh, tm, HEAD_DIM), lambda i: (0, i, 0))
    heads_t = pl.BlockSpec((ATTN_HEADS, HEAD_DIM, tm), lambda i: (0, 0, i))
    chunks_t = pl.BlockSpec((ATTN_KV_HEADS, 1, VT_ROWS, tm), lambda i: (0, i, 0, 0))
    table = pl.BlockSpec((tm, LANES), lambda i: (i % (seq_len // tm), 0))
    return pl.pallas_call(
        _inproj_kernel,
        grid=(n // tm,),
        in_specs=[row(d), vec(d), _vmem_spec(), vec(ATTN_Q_W), vec(ATTN_KV_W), _vmem_spec(),
                  table, table],
        out_specs=[heads_t, heads(ATTN_KV_HEADS), chunks_t,
                   row(RET_QK_W), row(RET_QK_W), row(RET_V_W), row(RET_V_W)],
        out_shape=[jax.ShapeDtypeStruct((ATTN_HEADS, HEAD_DIM, n), _BF16),
                   jax.ShapeDtypeStruct((ATTN_KV_HEADS, n, HEAD_DIM), _BF16),
                   jax.ShapeDtypeStruct((ATTN_KV_HEADS, n // tm, VT_ROWS, tm), _BF16),
                   jax.ShapeDtypeStruct((n, RET_QK_W), _F32),
                   jax.ShapeDtypeStruct((n, RET_QK_W), _F32),
                   jax.ShapeDtypeStruct((n, RET_V_W), _BF16),
                   jax.ShapeDtypeStruct((n, RET_V_W), _F32)],
        compiler_params=_params("parallel"),
        name="inproj",
    )(x, gain, w_in, q_gain, k_gain, seg, cos, sin)


def _attn_kernel(qt_ref, k_ref, vt_ref, o_ref, acc_ref, s0_ref, s1_ref, s2_ref, s3_ref, *, seq_len,
                 split):
    qt = jnp.concatenate([qt_ref[h] for h in range(ATTN_GROUP)], axis=1)
    tq = qt_ref.shape[2]
    nk = seq_len // ATTN_TK
    acc_ref[...] = jnp.zeros(acc_ref.shape, _F32)

    width = qt.shape[1] // split
    blocks = [slice(b * width, (b + 1) * width) for b in range(split)]

    def scores(j, dst_ref, m_prev, blk):
        start = pl.multiple_of(j * ATTN_TK, ATTN_TK)
        kj = k_ref[0, pl.ds(start, ATTN_TK), :]
        st = jnp.dot(kj, qt[:, blk], preferred_element_type=_F32)
        dst_ref[:, blk] = st
        m_old = m_prev[:, blk]
        m_next = jnp.maximum(m_old, jnp.max(st, axis=0, keepdims=True))
        return m_next, jnp.exp2(m_old - m_next)

    def consume(j, src_ref, m, alpha, blk):
        pt = jnp.exp2(src_ref[:, blk] - m[:, blk])
        pv = jnp.dot(vt_ref[0, j], pt.astype(_BF16), preferred_element_type=_F32)
        acc_ref[:, blk] = alpha[:, blk] * acc_ref[:, blk] + pv

    def chunk_step(j_score, dst_ref, m_prev, j_use=None, src_ref=None, m=None, alpha=None):
        parts = []
        for blk in blocks:
            if j_score is not None:
                parts.append(scores(j_score, dst_ref, m_prev, blk))
            if j_use is not None:
                consume(j_use, src_ref, m, alpha, blk)
        if j_score is None:
            return None, None
        return (jnp.concatenate([p[0] for p in parts], axis=1),
                jnp.concatenate([p[1] for p in parts], axis=1))

    bufs = (s0_ref, s1_ref, s2_ref, s3_ref)
    m0, a0 = chunk_step(0, bufs[0], jnp.full((1, qt.shape[1]), -jnp.inf, _F32))
    m1, a1 = chunk_step(1, bufs[1], m0)

    def half_step(j, carry, cur, nxt, lookahead):
        m_a, alpha_a, m_b, alpha_b = carry
        m_c, alpha_c = chunk_step(j + 2 if lookahead else None, nxt[0], m_b, j, cur[0], m_a, alpha_a)
        m_d, alpha_d = chunk_step(j + 3 if lookahead else None, nxt[1], m_c, j + 1, cur[1], m_b,
                                  alpha_b)
        return (m_c, alpha_c, m_d, alpha_d) if lookahead else None

    unroll = ATTN_UNROLL if nk - 4 >= 2 * ATTN_UNROLL else 4

    def body(i, carry):
        j = unroll * i
        for h in range(0, unroll, 4):
            carry = half_step(j + h, carry, bufs[:2], bufs[2:], True)
            carry = half_step(j + h + 2, carry, bufs[2:], bufs[:2], True)
        return carry

    n_body = (nk - 4) // unroll
    carry = lax.fori_loop(0, n_body, body, (m0, a0, m1, a1))
    for j in range(n_body * unroll, nk - 4, 4):
        carry = half_step(j, carry, bufs[:2], bufs[2:], True)
        carry = half_step(j + 2, carry, bufs[2:], bufs[:2], True)
    carry = half_step(nk - 4, carry, bufs[:2], bufs[2:], True)
    half_step(nk - 2, carry, bufs[2:], bufs[:2], False)
    out_t = acc_ref[:HEAD_DIM, :] / acc_ref[HEAD_DIM:HEAD_DIM + 1, :]
    pair = LANES // HEAD_DIM
    for c in range(ATTN_GROUP // pair):
        stacked = jnp.concatenate(
            [out_t[:, (pair * c + s) * tq:(pair * c + s + 1) * tq] for s in range(pair)], axis=0)
        o_ref[:, c * LANES:(c + 1) * LANES] = stacked.T.astype(o_ref.dtype)


def _attention(qt, k, vt, batch, seq_len):
    n = k.shape[1]
    nk = seq_len // ATTN_TK
    looped = nk - 4 >= 2 * ATTN_UNROLL
    tq = ATTN_TQ if looped else ATTN_TQ_SHORT
    assert n == batch * seq_len and seq_len % (4 * ATTN_TK) == 0 and seq_len % tq == 0
    nq = seq_len // tq
    cols = ATTN_GROUP * tq
    return pl.pallas_call(
        functools.partial(_attn_kernel, seq_len=seq_len, split=ATTN_SPLIT if looped else 1),
        grid=(batch, ATTN_KV_HEADS, nq),
        in_specs=[
            pl.BlockSpec((ATTN_GROUP, HEAD_DIM, tq), lambda b, g, i: (g, 0, b * nq + i)),
            pl.BlockSpec((1, seq_len, HEAD_DIM), lambda b, g, i: (g, b, 0)),
            pl.BlockSpec((1, nk, VT_ROWS, ATTN_TK), lambda b, g, i: (g, b, 0, 0)),
        ],
        out_specs=pl.BlockSpec((tq, ATTN_GROUP * HEAD_DIM), lambda b, g, i: (b * nq + i, g)),
        out_shape=jax.ShapeDtypeStruct((n, ATTN_Q_W), _BF16),
        scratch_shapes=[pltpu.VMEM((VT_ROWS, cols), _F32),
                        ] + [pltpu.VMEM((ATTN_TK, cols), _F32)] * 4,
        compiler_params=_params("parallel", "parallel", "arbitrary"),
        name=f"attn_t{seq_len}",
    )(qt, k, vt)


def _ret_kernel(dec_ref, qf_ref, kf_ref, vf_ref, qb_ref, kb_ref, vb_ref, of_ref, ob_ref,
                dmat_ref, qd_ref, kd_ref, cd_ref, s_ref):
    c = RET_CHUNK
    t = pl.program_id(1)

    @pl.when(t == 0)
    def _init():
        row = lax.broadcasted_iota(jnp.int32, (c, c), 0).astype(_F32)
        col = lax.broadcasted_iota(jnp.int32, (c, c), 1).astype(_F32)
        idx = lax.broadcasted_iota(jnp.int32, (c, LANES), 0).astype(_F32)
        for slot in range(2 * RET_HEADS):
            fwd = slot < RET_HEADS
            lg_c = jnp.log1p(-jnp.exp(dec_ref[slot:slot + 1, :]))
            lg = lg_c[:, :LANES]
            diff = (row - col) if fwd else (col - row)
            mask = (diff >= 0) if fwd else (diff > 0)
            dmat_ref[slot] = jnp.where(mask, jnp.exp(jnp.where(mask, diff, 0.0) * lg_c), 0.0)
            qd_ref[slot] = jnp.exp(((idx + 1.0) if fwd else (c - idx)) * lg)
            kd_ref[slot] = jnp.exp(((c - 1.0 - idx) if fwd else idx) * lg)
            cd_ref[slot] = jnp.exp(c * lg) * jnp.ones((8, LANES), _F32)
        s_ref[...] = jnp.zeros(s_ref.shape, _F32)

    slots = range(2 * RET_HEADS)
    n_chunks = qf_ref.shape[0] // c
    items = [(ci, s) for ci in range(n_chunks) for s in slots]
    qs, ks, vs, rows = {}, {}, {}, {}
    for ci, s in items:
        fwd = s < RET_HEADS
        hd = s % RET_HEADS
        r0 = (ci if fwd else n_chunks - 1 - ci) * c
        key_cols = slice(hd * RET_KEY_DIM, (hd + 1) * RET_KEY_DIM)
        qs[ci, s] = (qf_ref if fwd else qb_ref)[r0:r0 + c, key_cols]
        ks[ci, s] = (kf_ref if fwd else kb_ref)[r0:r0 + c, key_cols]
        vs[ci, s] = (vf_ref if fwd else vb_ref)[r0:r0 + c,
                                                hd * RET_VALUE_DIM:(hd + 1) * RET_VALUE_DIM]
        rows[ci, s] = r0
    qk = {i: lax.dot_general(qs[i].astype(_BF16), ks[i].astype(_BF16), _NT,
                             preferred_element_type=_F32) for i in items}
    kv = {i: lax.dot_general((ks[i] * kd_ref[i[1]][:, :RET_KEY_DIM]).astype(_BF16), vs[i], _TN,
                             preferred_element_type=_F32) for i in items}
    states = {}
    for s in slots:
        state = s_ref[s]
        for ci in range(n_chunks):
            states[ci, s] = state
            state = state * cd_ref[s][:1, :] + kv[ci, s]
        s_ref[s] = state
    for i in items:
        s = i[1]
        lhs = jnp.concatenate([(qk[i] * dmat_ref[s]).astype(_BF16),
                               (qs[i] * qd_ref[s][:, :RET_KEY_DIM]).astype(_BF16)], axis=1)
        rhs = jnp.concatenate([vs[i], states[i].astype(_BF16)], axis=0)
        out_ref = of_ref if s < RET_HEADS else ob_ref
        hd = s % RET_HEADS
        out_ref[rows[i]:rows[i] + c, hd * RET_VALUE_DIM:(hd + 1) * RET_VALUE_DIM] = jnp.dot(
            lhs, rhs, preferred_element_type=_F32)


def _retention(rq, rk, rv, dec, batch, seq_len):
    n = rq.shape[0]
    c = RET_CHUNK
    blk = RET_BLOCK
    assert n == batch * seq_len and seq_len % blk == 0 and blk % c == 0 and RET_VALUE_DIM == LANES
    nc = seq_len // blk
    fwd = lambda b, t: b * nc + t
    bwd = lambda b, t: b * nc + (nc - 1 - t)
    qk = lambda f: pl.BlockSpec((blk, RET_QK_W), lambda b, t: (f(b, t), 0))
    vv = lambda f: pl.BlockSpec((blk, RET_V_W), lambda b, t: (f(b, t), 0))
    slots = 2 * RET_HEADS
    return pl.pallas_call(
        _ret_kernel,
        grid=(batch, nc),
        in_specs=[pl.BlockSpec((slots, c), lambda b, t: (0, 0)),
                  qk(fwd), qk(fwd), vv(fwd), qk(bwd), qk(bwd), vv(bwd)],
        out_specs=[vv(fwd), vv(bwd)],
        out_shape=[jax.ShapeDtypeStruct((n, RET_V_W), _F32)] * 2,
        scratch_shapes=[pltpu.VMEM((slots, c, c), _F32),
                        pltpu.VMEM((slots, c, LANES), _F32),
                        pltpu.VMEM((slots, c, LANES), _F32),
                        pltpu.VMEM((slots, 8, LANES), _F32),
                        pltpu.VMEM((slots, RET_KEY_DIM, RET_VALUE_DIM), _F32)],
        compiler_params=_params("arbitrary", "arbitrary"),
        name=f"retention_t{seq_len}",
    )(dec, rq, rk, rv, rq, rk, rv)


def _merge_kernel(x_ref, g_ref, ao_ref, rf_ref, rb_ref, rg_ref, rn_ref, wg_ref, bg_ref,
                  wba_ref, wbr_ref, wo_ref, o_ref):
    x = x_ref[...]
    h = _rms_rows(x, g_ref[...]).astype(_BF16)
    gates = jax.nn.sigmoid(jnp.dot(h, wg_ref[...], preferred_element_type=_F32) + bg_ref[...])
    ya = jnp.dot(ao_ref[...], wba_ref[...], preferred_element_type=_F32)
    y = rf_ref[...] + rb_ref[...]
    cols = []
    for j in range(0, RET_V_W, RET_VALUE_DIM):
        yh = y[:, j:j + RET_VALUE_DIM]
        mu = jnp.mean(yh, axis=-1, keepdims=True)
        yc = yh - mu
        var = jnp.mean(yc * yc, axis=-1, keepdims=True)
        cols.append(yc * lax.rsqrt(var + NORM_EPS))
    yn = jnp.concatenate(cols, axis=1) * rn_ref[...]
    yr_in = (jax.nn.silu(rg_ref[...]) * yn).astype(_BF16)
    yr = jnp.dot(yr_in, wbr_ref[...], preferred_element_type=_F32)
    mixed = (gates[:, :D_MODEL] * ya + gates[:, D_MODEL:] * yr).astype(_BF16)
    o_ref[...] = x + jnp.dot(mixed, wo_ref[...], preferred_element_type=_F32)


def _merge(x, gain, ao, rf, rb, rg, ret_gain, w_gate, b_gate, w_ba, w_br, w_out):
    n, d = x.shape
    tm = TOKEN_TILE
    row = lambda w: pl.BlockSpec((tm, w), lambda i: (i, 0))
    vec = lambda w: pl.BlockSpec((1, w), lambda i: (0, 0))
    return pl.pallas_call(
        _merge_kernel,
        grid=(n // tm,),
        in_specs=[row(d), vec(d), row(ATTN_Q_W), row(RET_V_W), row(RET_V_W), row(RET_V_W),
                  vec(RET_V_W), _vmem_spec(), vec(2 * d), _vmem_spec(), _vmem_spec(), _vmem_spec()],
        out_specs=row(d),
        out_shape=jax.ShapeDtypeStruct((n, d), _F32),
        compiler_params=_params("parallel"),
        name="merge",
    )(x, gain, ao, rf, rb, rg, ret_gain, w_gate, b_gate, w_ba, w_br, w_out)


def _trunk(xs, p):
    depth = p["ffn1_w13"].shape[0]
    shapes = [x.shape[:2] for x in xs]
    xs = [x.reshape(-1, D_MODEL) for x in xs]
    cos, sin = _rope_tables(max(t for _, t in shapes))

    seg_id = jnp.arange(MXU_TILE) // HEAD_DIM
    seg = (seg_id[:, None] == seg_id[None, :]).astype(_BF16)
    bf = lambda w: w.astype(_BF16)
    vec = lambda g: g.astype(_F32)[None, :]
    final_gain = vec(p["final_norm"])

    for l in range(depth):
        ffn1 = (vec(p["ffn1_norm"][l]), bf(p["ffn1_w13"][l]), bf(p["ffn1_w2"][l]), final_gain)
        ffn2 = (vec(p["ffn2_norm"][l]), bf(p["ffn2_w13"][l]), bf(p["ffn2_w2"][l]), final_gain)
        mix_gain = vec(p["mix_norm"][l])
        proj = (mix_gain, bf(p["w_in"][l]), vec(jnp.tile(p["q_norm"][l], ATTN_HEADS)),
                vec(jnp.tile(p["k_norm"][l], ATTN_KV_HEADS)), seg, cos, sin)
        dec = jnp.concatenate([p["ret_decay_fwd"][l], p["ret_decay_bwd"][l]]).astype(_F32)
        dec = jnp.broadcast_to(dec[:, None], (2 * RET_HEADS, RET_CHUNK))
        merge = (vec(p["ret_norm"][l]), bf(p["w_gate"][l]), vec(p["b_gate"][l]),
                 bf(p["w_branch_attn"][l]), bf(p["w_branch_ret"][l]), bf(p["w_out"][l]))
        for i, (b, t) in enumerate(shapes):
            x = _ffn(xs[i], *ffn1, False)
            q, k, v, rq, rk, rv, rg = _inproj(x, *proj, t)
            ao = _attention(q, k, v, b, t)
            rf, rb = _retention(rq, rk, rv, dec, b, t)
            x = _merge(x, mix_gain, ao, rf, rb, rg, *merge)
            xs[i] = _ffn(x, *ffn2, l == depth - 1)

    return [x.reshape(b, t, D_MODEL) for x, (b, t) in zip(xs, shapes)]


def kernel(x_prompt, x_sample, ffn1_norm, ffn1_w13, ffn1_w2, mix_norm, w_in, q_norm, k_norm,
           ret_decay_fwd, ret_decay_bwd, ret_norm, w_branch_attn, w_branch_ret, w_gate, b_gate,
           w_out, ffn2_norm, ffn2_w13, ffn2_w2, final_norm):
    p = dict(ffn1_norm=ffn1_norm, ffn1_w13=ffn1_w13, ffn1_w2=ffn1_w2, mix_norm=mix_norm, w_in=w_in,
             q_norm=q_norm, k_norm=k_norm, ret_decay_fwd=ret_decay_fwd, ret_decay_bwd=ret_decay_bwd,
             ret_norm=ret_norm, w_branch_attn=w_branch_attn, w_branch_ret=w_branch_ret, w_gate=w_gate,
             b_gate=b_gate, w_out=w_out, ffn2_norm=ffn2_norm, ffn2_w13=ffn2_w13, ffn2_w2=ffn2_w2,
             final_norm=final_norm)
    y_prompt, y_sample = _trunk([x_prompt, x_sample], p)
    return (y_prompt, y_sample)
```

```python
import functools
import math

import jax
import jax.numpy as jnp
from jax import lax
from jax.experimental import pallas as pl
from jax.experimental.pallas import tpu as pltpu

D_MODEL = 1024
GRID_W = 64
HEAD_DIM = 64
ATTN_HEADS = 8
ATTN_KV_HEADS = 2
ATTN_GROUP = ATTN_HEADS // ATTN_KV_HEADS
RET_HEADS = 4
RET_KEY_DIM = HEAD_DIM
RET_VALUE_DIM = 2 * RET_KEY_DIM
D_FF = 2816
ROPE_THETA = 10000.0
ROPE_AXIS_PAIRS = HEAD_DIM // 4
NORM_EPS = 1e-6
ATTN_Q_W = ATTN_HEADS * HEAD_DIM
ATTN_KV_W = ATTN_KV_HEADS * HEAD_DIM
RET_QK_W = RET_HEADS * RET_KEY_DIM
RET_V_W = RET_HEADS * RET_VALUE_DIM
IN_PROJ_W = ATTN_Q_W + 2 * ATTN_KV_W + 2 * RET_QK_W + 2 * RET_V_W

LANES = 128
MXU_TILE = 256
VMEM_LIMIT = 56 * 1024 * 1024

TOKEN_TILE = 512
FF_CHUNK = MXU_TILE
ATTN_TQ = 256
ATTN_TQ_SHORT = 512
ATTN_TK = 512
ATTN_TK_SHORT = 256
VT_CHUNK = 256
VT_ROWS = HEAD_DIM + 64
ATTN_UNROLL = 8
ATTN_LANE_BLOCK = 512
RET_CHUNK = 128
RET_BLOCK = 512
ROPE_ROWS = 1024

_BF16 = jnp.bfloat16
_F32 = jnp.float32
_NT = (((1,), (1,)), ((), ()))
_TN = (((0,), (0,)), ((), ()))


def _vmem_spec():
    return pl.BlockSpec(memory_space=pltpu.VMEM)


def _params(*sem):
    return pltpu.CompilerParams(dimension_semantics=sem, vmem_limit_bytes=VMEM_LIMIT)


def _rms_rows(x, gain):
    ms = jnp.mean(x * x, axis=-1, keepdims=True)
    return x * lax.rsqrt(ms + NORM_EPS) * gain


def _rope_table_kernel(freq_ref, cos_ref, sin_ref):
    rows = cos_ref.shape[0]
    t = pl.program_id(0) * rows + lax.broadcasted_iota(jnp.int32, (rows, LANES), 0)
    lane = lax.broadcasted_iota(jnp.int32, (rows, LANES), 1)
    pair = lane & (2 * ROPE_AXIS_PAIRS - 1)
    shift = GRID_W.bit_length() - 1
    pos = jnp.where(pair < ROPE_AXIS_PAIRS, t >> shift, t & (GRID_W - 1)).astype(_F32)
    ang = pos * freq_ref[...]
    sign = jnp.where((lane & (HEAD_DIM - 1)) < HEAD_DIM // 2, -1.0, 1.0).astype(_F32)
    cos_ref[...] = jnp.cos(ang)
    sin_ref[...] = jnp.sin(ang) * sign


def _rope_tables(t_max):
    assert GRID_W & (GRID_W - 1) == 0 and t_max % ROPE_ROWS == 0
    freqs = ROPE_THETA ** (-jnp.arange(ROPE_AXIS_PAIRS, dtype=_F32) / ROPE_AXIS_PAIRS)
    freq_lanes = jnp.tile(freqs, LANES // ROPE_AXIS_PAIRS)[None, :]
    return pl.pallas_call(
        _rope_table_kernel,
        grid=(t_max // ROPE_ROWS,),
        in_specs=[pl.BlockSpec((1, LANES), lambda i: (0, 0))],
        out_specs=[pl.BlockSpec((ROPE_ROWS, LANES), lambda i: (i, 0))] * 2,
        out_shape=[jax.ShapeDtypeStruct((t_max, LANES), _F32)] * 2,
        compiler_params=_params("parallel"),
        name="rope_tables",
    )(freq_lanes)


def _ffn_kernel(x_ref, g_ref, w13_ref, w2_ref, fg_ref, o_ref, acc_ref, *, final_norm):
    x = x_ref[...]
    h = _rms_rows(x, g_ref[...]).astype(_BF16)
    for c in range(D_FF // FF_CHUNK):
        lo = c * FF_CHUNK
        a = jnp.dot(h, w13_ref[:, lo:lo + FF_CHUNK], preferred_element_type=_F32)
        b = jnp.dot(h, w13_ref[:, D_FF + lo:D_FF + lo + FF_CHUNK], preferred_element_type=_F32)
        act = (jax.nn.silu(a) * b).astype(_BF16)
        part = jnp.dot(act, w2_ref[lo:lo + FF_CHUNK, :], preferred_element_type=_F32)
        if c == 0:
            acc_ref[...] = part
        else:
            acc_ref[...] += part
    y = x + 0.5 * acc_ref[...]
    if final_norm:
        y = _rms_rows(y, fg_ref[...])
    o_ref[...] = y


def _ffn(x, gain, w13, w2, final_gain, final_norm):
    n, d = x.shape
    assert n % TOKEN_TILE == 0 and D_FF % FF_CHUNK == 0
    row = pl.BlockSpec((TOKEN_TILE, d), lambda i: (i, 0))
    vec = pl.BlockSpec((1, d), lambda i: (0, 0))
    return pl.pallas_call(
        functools.partial(_ffn_kernel, final_norm=final_norm),
        grid=(n // TOKEN_TILE,),
        in_specs=[row, vec, _vmem_spec(), _vmem_spec(), vec],
        out_specs=row,
        out_shape=jax.ShapeDtypeStruct((n, d), _F32),
        scratch_shapes=[pltpu.VMEM((TOKEN_TILE, d), _F32)],
        compiler_params=_params("parallel"),
        name="ffn",
    )(x, gain, w13, w2, final_gain)


def _segment_sum(sq, seg):
    hi = sq.astype(_BF16)
    lo = (sq - hi.astype(_F32)).astype(_BF16)
    return (jnp.dot(hi, seg, preferred_element_type=_F32)
            + jnp.dot(lo, seg, preferred_element_type=_F32))


def _rope_lanes(x, cos, sin, first_half):
    partner = jnp.where(first_half, pltpu.roll(x, LANES - HEAD_DIM // 2, axis=1),
                        pltpu.roll(x, HEAD_DIM // 2, axis=1))
    return x * cos + partner * sin


def _inproj_kernel(x_ref, g_ref, w_ref, qg_ref, kg_ref, seg_ref, cos_ref, sin_ref,
                   q_ref, k_ref, v_ref, rq_ref, rk_ref, rv_ref, rg_ref):
    h = _rms_rows(x_ref[...], g_ref[...]).astype(_BF16)
    cos = cos_ref[...]
    sin = sin_ref[...]
    lane = lax.broadcasted_iota(jnp.int32, cos.shape, 1)
    first_half = (lane & (HEAD_DIM - 1)) < HEAD_DIM // 2
    seg = seg_ref[...]
    inv_hd = 1.0 / HEAD_DIM

    def head_norm_rope(xcols, gain, seg_block):
        ss = _segment_sum(xcols * xcols, seg_block)
        xn = xcols * lax.rsqrt(ss * inv_hd + NORM_EPS) * gain
        return [_rope_lanes(xn[:, j:j + LANES], cos, sin, first_half)
                for j in range(0, xn.shape[1], LANES)]

    offset = [0]

    def project(width):
        lo = offset[0]
        offset[0] += width
        return jnp.dot(h, w_ref[:, lo:lo + width], preferred_element_type=_F32)

    scale = HEAD_DIM ** -0.5 * math.log2(math.e)
    heads_per_block = LANES // HEAD_DIM
    aq = project(ATTN_Q_W)
    qcols = []
    for j in range(0, ATTN_Q_W, MXU_TILE):
        qcols += head_norm_rope(aq[:, j:j + MXU_TILE], qg_ref[:, j:j + MXU_TILE], seg)
    for j, col in enumerate(qcols):
        col_t = (col * scale).T.astype(_BF16)
        for s in range(heads_per_block):
            q_ref[heads_per_block * j + s] = col_t[s * HEAD_DIM:(s + 1) * HEAD_DIM]
    akv = project(2 * ATTN_KV_W)
    ak, av = akv[:, :ATTN_KV_W], akv[:, ATTN_KV_W:]
    kcols = head_norm_rope(ak, kg_ref[...], seg[:ATTN_KV_W, :ATTN_KV_W])
    for j, col in enumerate(kcols):
        col = col.astype(_BF16)
        for s in range(heads_per_block):
            k_ref[heads_per_block * j + s] = col[:, s * HEAD_DIM:(s + 1) * HEAD_DIM]
    av_t = av.T.astype(_BF16)
    ones = jnp.ones((VT_ROWS - HEAD_DIM, VT_CHUNK), _BF16)
    for g in range(ATTN_KV_HEADS):
        for c in range(av_t.shape[1] // VT_CHUNK):
            v_ref[g, c, :HEAD_DIM] = av_t[g * HEAD_DIM:(g + 1) * HEAD_DIM,
                                          c * VT_CHUNK:(c + 1) * VT_CHUNK]
            v_ref[g, c, HEAD_DIM:] = ones

    rq = project(RET_QK_W)
    for j in range(0, RET_QK_W, LANES):
        rq_ref[:, j:j + LANES] = _rope_lanes(rq[:, j:j + LANES], cos, sin, first_half)
    rk = project(RET_QK_W)
    for j in range(0, RET_QK_W, LANES):
        rk_ref[:, j:j + LANES] = (_rope_lanes(rk[:, j:j + LANES], cos, sin, first_half)
                                  * (RET_KEY_DIM ** -0.5))
    rv_ref[...] = project(RET_V_W).astype(_BF16)
    rg_ref[...] = project(RET_V_W)


def _inproj(x, gain, w_in, q_gain, k_gain, seg, cos, sin, seq_len):
    n, d = x.shape
    tm = TOKEN_TILE
    assert n % tm == 0 and seq_len % tm == 0 and tm % VT_CHUNK == 0
    row = lambda w: pl.BlockSpec((tm, w), lambda i: (i, 0))
    vec = lambda w: pl.BlockSpec((1, w), lambda i: (0, 0))
    heads = lambda nh: pl.BlockSpec((nh, tm, HEAD_DIM), lambda i: (0, i, 0))
    heads_t = pl.BlockSpec((ATTN_HEADS, HEAD_DIM, tm), lambda i: (0, 0, i))
    chunks_t = pl.BlockSpec((ATTN_KV_HEADS, tm // VT_CHUNK, VT_ROWS, VT_CHUNK),
                            lambda i: (0, i, 0, 0))
    table = pl.BlockSpec((tm, LANES), lambda i: (i % (seq_len // tm), 0))
    return pl.pallas_call(
        _inproj_kernel,
        grid=(n // tm,),
        in_specs=[row(d), vec(d), _vmem_spec(), vec(ATTN_Q_W), vec(ATTN_KV_W), _vmem_spec(),
                  table, table],
        out_specs=[heads_t, heads(ATTN_KV_HEADS), chunks_t,
                   row(RET_QK_W), row(RET_QK_W), row(RET_V_W), row(RET_V_W)],
        out_shape=[jax.ShapeDtypeStruct((ATTN_HEADS, HEAD_DIM, n), _BF16),
                   jax.ShapeDtypeStruct((ATTN_KV_HEADS, n, HEAD_DIM), _BF16),
                   jax.ShapeDtypeStruct((ATTN_KV_HEADS, n // VT_CHUNK, VT_ROWS, VT_CHUNK), _BF16),
                   jax.ShapeDtypeStruct((n, RET_QK_W), _F32),
                   jax.ShapeDtypeStruct((n, RET_QK_W), _F32),
                   jax.ShapeDtypeStruct((n, RET_V_W), _BF16),
                   jax.ShapeDtypeStruct((n, RET_V_W), _F32)],
        compiler_params=_params("parallel"),
        name="inproj",
    )(x, gain, w_in, q_gain, k_gain, seg, cos, sin)


def _attn_kernel(qt_ref, k_ref, vt_ref, o_ref, acc_ref, s0_ref, s1_ref, s2_ref, s3_ref, *, seq_len,
                 split):
    qt = jnp.concatenate([qt_ref[h] for h in range(ATTN_GROUP)], axis=1)
    tq = qt_ref.shape[2]
    tk = s0_ref.shape[0]
    nk = seq_len // tk
    sub = tk // VT_CHUNK
    acc_ref[...] = jnp.zeros(acc_ref.shape, _F32)

    width = qt.shape[1] // split
    blocks = [slice(b * width, (b + 1) * width) for b in range(split)]

    def scores(j, dst_ref, m_prev, blk):
        start = pl.multiple_of(j * tk, tk)
        kj = k_ref[0, pl.ds(start, tk), :]
        st = jnp.dot(kj, qt[:, blk], preferred_element_type=_F32)
        dst_ref[:, blk] = st
        m_old = m_prev[:, blk]
        m_next = jnp.maximum(m_old, jnp.max(st, axis=0, keepdims=True))
        return m_next, jnp.exp2(m_old - m_next)

    def consume(j, src_ref, m, alpha, blk):
        pt = jnp.exp2(src_ref[:, blk] - m[:, blk])
        vt = jnp.concatenate([vt_ref[0, sub * j + u] for u in range(sub)], axis=1)
        pv = jnp.dot(vt, pt.astype(_BF16), preferred_element_type=_F32)
        acc_ref[:, blk] = alpha[:, blk] * acc_ref[:, blk] + pv

    def chunk_step(j_score, dst_ref, m_prev, j_use=None, src_ref=None, m=None, alpha=None):
        parts = []
        for blk in blocks:
            if j_score is not None:
                parts.append(scores(j_score, dst_ref, m_prev, blk))
            if j_use is not None:
                consume(j_use, src_ref, m, alpha, blk)
        if j_score is None:
            return None, None
        return (jnp.concatenate([p[0] for p in parts], axis=1),
                jnp.concatenate([p[1] for p in parts], axis=1))

    bufs = (s0_ref, s1_ref, s2_ref, s3_ref)
    m0, a0 = chunk_step(0, bufs[0], jnp.full((1, qt.shape[1]), -jnp.inf, _F32))
    m1, a1 = chunk_step(1, bufs[1], m0)

    def half_step(j, carry, cur, nxt, lookahead):
        m_a, alpha_a, m_b, alpha_b = carry
        m_c, alpha_c = chunk_step(j + 2 if lookahead else None, nxt[0], m_b, j, cur[0], m_a, alpha_a)
        m_d, alpha_d = chunk_step(j + 3 if lookahead else None, nxt[1], m_c, j + 1, cur[1], m_b,
                                  alpha_b)
        return (m_c, alpha_c, m_d, alpha_d) if lookahead else None

    unroll = ATTN_UNROLL if nk - 4 >= 2 * ATTN_UNROLL else 4

    def body(i, carry):
        j = unroll * i
        for h in range(0, unroll, 4):
            carry = half_step(j + h, carry, bufs[:2], bufs[2:], True)
            carry = half_step(j + h + 2, carry, bufs[2:], bufs[:2], True)
        return carry

    n_body = (nk - 4) // unroll
    carry = lax.fori_loop(0, n_body, body, (m0, a0, m1, a1))
    for j in range(n_body * unroll, nk - 4, 4):
        carry = half_step(j, carry, bufs[:2], bufs[2:], True)
        carry = half_step(j + 2, carry, bufs[2:], bufs[:2], True)
    carry = half_step(nk - 4, carry, bufs[:2], bufs[2:], True)
    half_step(nk - 2, carry, bufs[2:], bufs[:2], False)
    out_t = acc_ref[:HEAD_DIM, :] / acc_ref[HEAD_DIM:HEAD_DIM + 1, :]
    pair = LANES // HEAD_DIM
    for c in range(ATTN_GROUP // pair):
        stacked = jnp.concatenate(
            [out_t[:, (pair * c + s) * tq:(pair * c + s + 1) * tq] for s in range(pair)], axis=0)
        o_ref[:, c * LANES:(c + 1) * LANES] = stacked.T.astype(o_ref.dtype)


def _attention(qt, k, vt, batch, seq_len):
    n = k.shape[1]
    short = seq_len // ATTN_TK - 4 < 2 * ATTN_UNROLL
    tq, tk = (ATTN_TQ_SHORT, ATTN_TK_SHORT) if short else (ATTN_TQ, ATTN_TK)
    assert n == batch * seq_len and seq_len % (4 * tk) == 0 and seq_len % tq == 0
    assert tk % VT_CHUNK == 0
    nq = seq_len // tq
    cols = ATTN_GROUP * tq
    return pl.pallas_call(
        functools.partial(_attn_kernel, seq_len=seq_len, split=cols // ATTN_LANE_BLOCK),
        grid=(batch, ATTN_KV_HEADS, nq),
        in_specs=[
            pl.BlockSpec((ATTN_GROUP, HEAD_DIM, tq), lambda b, g, i: (g, 0, b * nq + i)),
            pl.BlockSpec((1, seq_len, HEAD_DIM), lambda b, g, i: (g, b, 0)),
            pl.BlockSpec((1, seq_len // VT_CHUNK, VT_ROWS, VT_CHUNK), lambda b, g, i: (g, b, 0, 0)),
        ],
        out_specs=pl.BlockSpec((tq, ATTN_GROUP * HEAD_DIM), lambda b, g, i: (b * nq + i, g)),
        out_shape=jax.ShapeDtypeStruct((n, ATTN_Q_W), _BF16),
        scratch_shapes=[pltpu.VMEM((VT_ROWS, cols), _F32),
                        ] + [pltpu.VMEM((tk, cols), _F32)] * 4,
        compiler_params=_params("parallel", "parallel", "arbitrary"),
        name=f"attn_t{seq_len}",
    )(qt, k, vt)


def _ret_kernel(dec_ref, qf_ref, kf_ref, vf_ref, qb_ref, kb_ref, vb_ref, of_ref, ob_ref,
                dmat_ref, qd_ref, kd_ref, cd_ref, s_ref):
    c = RET_CHUNK
    t = pl.program_id(1)

    @pl.when(t == 0)
    def _init():
        row = lax.broadcasted_iota(jnp.int32, (c, c), 0).astype(_F32)
        col = lax.broadcasted_iota(jnp.int32, (c, c), 1).astype(_F32)
        idx = lax.broadcasted_iota(jnp.int32, (c, LANES), 0).astype(_F32)
        for slot in range(2 * RET_HEADS):
            fwd = slot < RET_HEADS
            lg_c = jnp.log1p(-jnp.exp(dec_ref[slot:slot + 1, :]))
            lg = lg_c[:, :LANES]
            diff = (row - col) if fwd else (col - row)
            mask = (diff >= 0) if fwd else (diff > 0)
            dmat_ref[slot] = jnp.where(mask, jnp.exp(jnp.where(mask, diff, 0.0) * lg_c), 0.0)
            qd_ref[slot] = jnp.exp(((idx + 1.0) if fwd else (c - idx)) * lg)
            kd_ref[slot] = jnp.exp(((c - 1.0 - idx) if fwd else idx) * lg)
            cd_ref[slot] = jnp.exp(c * lg) * jnp.ones((8, LANES), _F32)
        s_ref[...] = jnp.zeros(s_ref.shape, _F32)

    slots = range(2 * RET_HEADS)
    n_chunks = qf_ref.shape[0] // c
    items = [(ci, s) for ci in range(n_chunks) for s in slots]
    qs, ks, vs, rows = {}, {}, {}, {}
    for ci, s in items:
        fwd = s < RET_HEADS
        hd = s % RET_HEADS
        r0 = (ci if fwd else n_chunks - 1 - ci) * c
        key_cols = slice(hd * RET_KEY_DIM, (hd + 1) * RET_KEY_DIM)
        qs[ci, s] = (qf_ref if fwd else qb_ref)[r0:r0 + c, key_cols]
        ks[ci, s] = (kf_ref if fwd else kb_ref)[r0:r0 + c, key_cols]
        vs[ci, s] = (vf_ref if fwd else vb_ref)[r0:r0 + c,
                                                hd * RET_VALUE_DIM:(hd + 1) * RET_VALUE_DIM]
        rows[ci, s] = r0
    qk = {i: lax.dot_general(qs[i].astype(_BF16), ks[i].astype(_BF16), _NT,
                             preferred_element_type=_F32) for i in items}
    kv = {i: lax.dot_general((ks[i] * kd_ref[i[1]][:, :RET_KEY_DIM]).astype(_BF16), vs[i], _TN,
                             preferred_element_type=_F32) for i in items}
    states = {}
    for s in slots:
        state = s_ref[s]
        for ci in range(n_chunks):
            states[ci, s] = state
            state = state * cd_ref[s][:1, :] + kv[ci, s]
        s_ref[s] = state
    for i in items:
        s = i[1]
        lhs = jnp.concatenate([(qk[i] * dmat_ref[s]).astype(_BF16),
                               (qs[i] * qd_ref[s][:, :RET_KEY_DIM]).astype(_BF16)], axis=1)
        rhs = jnp.concatenate([vs[i], states[i].astype(_BF16)], axis=0)
        out_ref = of_ref if s < RET_HEADS else ob_ref
        hd = s % RET_HEADS
        out_ref[rows[i]:rows[i] + c, hd * RET_VALUE_DIM:(hd + 1) * RET_VALUE_DIM] = jnp.dot(
            lhs, rhs, preferred_element_type=_F32)


def _retention(rq, rk, rv, dec, batch, seq_len):
    n = rq.shape[0]
    c = RET_CHUNK
    blk = RET_BLOCK
    assert n == batch * seq_len and seq_len % blk == 0 and blk % c == 0 and RET_VALUE_DIM == LANES
    nc = seq_len // blk
    fwd = lambda b, t: b * nc + t
    bwd = lambda b, t: b * nc + (nc - 1 - t)
    qk = lambda f: pl.BlockSpec((blk, RET_QK_W), lambda b, t: (f(b, t), 0))
    vv = lambda f: pl.BlockSpec((blk, RET_V_W), lambda b, t: (f(b, t), 0))
    slots = 2 * RET_HEADS
    return pl.pallas_call(
        _ret_kernel,
        grid=(batch, nc),
        in_specs=[pl.BlockSpec((slots, c), lambda b, t: (0, 0)),
                  qk(fwd), qk(fwd), vv(fwd), qk(bwd), qk(bwd), vv(bwd)],
        out_specs=[vv(fwd), vv(bwd)],
        out_shape=[jax.ShapeDtypeStruct((n, RET_V_W), _F32)] * 2,
        scratch_shapes=[pltpu.VMEM((slots, c, c), _F32),
                        pltpu.VMEM((slots, c, LANES), _F32),
                        pltpu.VMEM((slots, c, LANES), _F32),
                        pltpu.VMEM((slots, 8, LANES), _F32),
                        pltpu.VMEM((slots, RET_KEY_DIM, RET_VALUE_DIM), _F32)],
        compiler_params=_params("arbitrary", "arbitrary"),
        name=f"retention_t{seq_len}",
    )(dec, rq, rk, rv, rq, rk, rv)


def _merge_kernel(x_ref, g_ref, ao_ref, rf_ref, rb_ref, rg_ref, rn_ref, wg_ref, bg_ref,
                  wba_ref, wbr_ref, wo_ref, o_ref):
    x = x_ref[...]
    h = _rms_rows(x, g_ref[...]).astype(_BF16)
    gates = jax.nn.sigmoid(jnp.dot(h, wg_ref[...], preferred_element_type=_F32) + bg_ref[...])
    ya = jnp.dot(ao_ref[...], wba_ref[...], preferred_element_type=_F32)
    y = rf_ref[...] + rb_ref[...]
    cols = []
    for j in range(0, RET_V_W, RET_VALUE_DIM):
        yh = y[:, j:j + RET_VALUE_DIM]
        mu = jnp.mean(yh, axis=-1, keepdims=True)
        yc = yh - mu
        var = jnp.mean(yc * yc, axis=-1, keepdims=True)
        cols.append(yc * lax.rsqrt(var + NORM_EPS))
    yn = jnp.concatenate(cols, axis=1) * rn_ref[...]
    yr_in = (jax.nn.silu(rg_ref[...]) * yn).astype(_BF16)
    yr = jnp.dot(yr_in, wbr_ref[...], preferred_element_type=_F32)
    mixed = (gates[:, :D_MODEL] * ya + gates[:, D_MODEL:] * yr).astype(_BF16)
    o_ref[...] = x + jnp.dot(mixed, wo_ref[...], preferred_element_type=_F32)


def _merge(x, gain, ao, rf, rb, rg, ret_gain, w_gate, b_gate, w_ba, w_br, w_out):
    n, d = x.shape
    tm = TOKEN_TILE
    row = lambda w: pl.BlockSpec((tm, w), lambda i: (i, 0))
    vec = lambda w: pl.BlockSpec((1, w), lambda i: (0, 0))
    return pl.pallas_call(
        _merge_kernel,
        grid=(n // tm,),
        in_specs=[row(d), vec(d), row(ATTN_Q_W), row(RET_V_W), row(RET_V_W), row(RET_V_W),
                  vec(RET_V_W), _vmem_spec(), vec(2 * d), _vmem_spec(), _vmem_spec(), _vmem_spec()],
        out_specs=row(d),
        out_shape=jax.ShapeDtypeStruct((n, d), _F32),
        compiler_params=_params("parallel"),
        name="merge",
    )(x, gain, ao, rf, rb, rg, ret_gain, w_gate, b_gate, w_ba, w_br, w_out)


def _trunk(xs, p):
    depth = p["ffn1_w13"].shape[0]
    shapes = [x.shape[:2] for x in xs]
    xs = [x.reshape(-1, D_MODEL) for x in xs]
    cos, sin = _rope_tables(max(t for _, t in shapes))

    seg_id = jnp.arange(MXU_TILE) // HEAD_DIM
    seg = (seg_id[:, None] == seg_id[None, :]).astype(_BF16)
    bf = lambda w: w.astype(_BF16)
    vec = lambda g: g.astype(_F32)[None, :]
    final_gain = vec(p["final_norm"])

    for l in range(depth):
        ffn1 = (vec(p["ffn1_norm"][l]), bf(p["ffn1_w13"][l]), bf(p["ffn1_w2"][l]), final_gain)
        ffn2 = (vec(p["ffn2_norm"][l]), bf(p["ffn2_w13"][l]), bf(p["ffn2_w2"][l]), final_gain)
        mix_gain = vec(p["mix_norm"][l])
        proj = (mix_gain, bf(p["w_in"][l]), vec(jnp.tile(p["q_norm"][l], ATTN_HEADS)),
                vec(jnp.tile(p["k_norm"][l], ATTN_KV_HEADS)), seg, cos, sin)
        dec = jnp.concatenate([p["ret_decay_fwd"][l], p["ret_decay_bwd"][l]]).astype(_F32)
        dec = jnp.broadcast_to(dec[:, None], (2 * RET_HEADS, RET_CHUNK))
        merge = (vec(p["ret_norm"][l]), bf(p["w_gate"][l]), vec(p["b_gate"][l]),
                 bf(p["w_branch_attn"][l]), bf(p["w_branch_ret"][l]), bf(p["w_out"][l]))
        for i, (b, t) in enumerate(shapes):
            x = _ffn(xs[i], *ffn1, False)
            q, k, v, rq, rk, rv, rg = _inproj(x, *proj, t)
            ao = _attention(q, k, v, b, t)
            rf, rb = _retention(rq, rk, rv, dec, b, t)
            x = _merge(x, mix_gain, ao, rf, rb, rg, *merge)
            xs[i] = _ffn(x, *ffn2, l == depth - 1)

    return [x.reshape(b, t, D_MODEL) for x, (b, t) in zip(xs, shapes)]


def kernel(x_prompt, x_sample, ffn1_norm, ffn1_w13, ffn1_w2, mix_norm, w_in, q_norm, k_norm,
           ret_decay_fwd, ret_decay_bwd, ret_norm, w_branch_attn, w_branch_ret, w_gate, b_gate,
           w_out, ffn2_norm, ffn2_w13, ffn2_w2, final_norm):
    p = dict(ffn1_norm=ffn1_norm, ffn1_w13=ffn1_w13, ffn1_w2=ffn1_w2, mix_norm=mix_norm, w_in=w_in,
             q_norm=q_norm, k_norm=k_norm, ret_decay_fwd=ret_decay_fwd, ret_decay_bwd=ret_decay_bwd,
             ret_norm=ret_norm, w_branch_attn=w_branch_attn, w_branch_ret=w_branch_ret, w_gate=w_gate,
             b_gate=b_gate, w_out=w_out, ffn2_norm=ffn2_norm, ffn2_w13=ffn2_w13, ffn2_w2=ffn2_w2,
             final_norm=final_norm)
    y_prompt, y_sample = _trunk([x_prompt, x_sample], p)
    return (y_prompt, y_sample)
```

```python
import functools
import math

import jax
import jax.numpy as jnp
from jax import lax
from jax.experimental import pallas as pl
from jax.experimental.pallas import tpu as pltpu

D_MODEL = 1024
GRID_W = 64
HEAD_DIM = 64
ATTN_HEADS = 8
ATTN_KV_HEADS = 2
ATTN_GROUP = ATTN_HEADS // ATTN_KV_HEADS
RET_HEADS = 4
RET_KEY_DIM = HEAD_DIM
RET_VALUE_DIM = 2 * RET_KEY_DIM
D_FF = 2816
ROPE_THETA = 10000.0
ROPE_AXIS_PAIRS = HEAD_DIM // 4
NORM_EPS = 1e-6
ATTN_Q_W = ATTN_HEADS * HEAD_DIM
ATTN_KV_W = ATTN_KV_HEADS * HEAD_DIM
RET_QK_W = RET_HEADS * RET_KEY_DIM
RET_V_W = RET_HEADS * RET_VALUE_DIM
IN_PROJ_W = ATTN_Q_W + 2 * ATTN_KV_W + 2 * RET_QK_W + 2 * RET_V_W

LANES = 128
MXU_TILE = 256
VMEM_LIMIT = 56 * 1024 * 1024

TOKEN_TILE = 512
FF_CHUNK = MXU_TILE
ATTN_LONG = (256, 512, HEAD_DIM + 64)
ATTN_SHORT = (512, 256, HEAD_DIM + 16)
VT_CHUNK = 256
ATTN_UNROLL = 8
ATTN_LANE_BLOCK = 512
RET_CHUNK = 128
RET_BLOCK = 512
ROPE_ROWS = 1024

_BF16 = jnp.bfloat16
_F32 = jnp.float32
_NT = (((1,), (1,)), ((), ()))
_TN = (((0,), (0,)), ((), ()))


def _vmem_spec():
    return pl.BlockSpec(memory_space=pltpu.VMEM)


def _params(*sem):
    return pltpu.CompilerParams(dimension_semantics=sem, vmem_limit_bytes=VMEM_LIMIT)


def _attn_tiling(seq_len):
    short = seq_len // ATTN_LONG[1] - 4 < 2 * ATTN_UNROLL
    return ATTN_SHORT if short else ATTN_LONG


def _rms_rows(x, gain):
    ms = jnp.mean(x * x, axis=-1, keepdims=True)
    return x * lax.rsqrt(ms + NORM_EPS) * gain


def _rope_table_kernel(freq_ref, cos_ref, sin_ref):
    rows = cos_ref.shape[0]
    t = pl.program_id(0) * rows + lax.broadcasted_iota(jnp.int32, (rows, LANES), 0)
    lane = lax.broadcasted_iota(jnp.int32, (rows, LANES), 1)
    pair = lane & (2 * ROPE_AXIS_PAIRS - 1)
    shift = GRID_W.bit_length() - 1
    pos = jnp.where(pair < ROPE_AXIS_PAIRS, t >> shift, t & (GRID_W - 1)).astype(_F32)
    ang = pos * freq_ref[...]
    sign = jnp.where((lane & (HEAD_DIM - 1)) < HEAD_DIM // 2, -1.0, 1.0).astype(_F32)
    cos_ref[...] = jnp.cos(ang)
    sin_ref[...] = jnp.sin(ang) * sign


def _rope_tables(t_max):
    assert GRID_W & (GRID_W - 1) == 0 and t_max % ROPE_ROWS == 0
    freqs = ROPE_THETA ** (-jnp.arange(ROPE_AXIS_PAIRS, dtype=_F32) / ROPE_AXIS_PAIRS)
    freq_lanes = jnp.tile(freqs, LANES // ROPE_AXIS_PAIRS)[None, :]
    return pl.pallas_call(
        _rope_table_kernel,
        grid=(t_max // ROPE_ROWS,),
        in_specs=[pl.BlockSpec((1, LANES), lambda i: (0, 0))],
        out_specs=[pl.BlockSpec((ROPE_ROWS, LANES), lambda i: (i, 0))] * 2,
        out_shape=[jax.ShapeDtypeStruct((t_max, LANES), _F32)] * 2,
        compiler_params=_params("parallel"),
        name="rope_tables",
    )(freq_lanes)


def _ffn_kernel(x_ref, g_ref, w13_ref, w2_ref, fg_ref, o_ref, acc_ref, *, final_norm):
    x = x_ref[...]
    h = _rms_rows(x, g_ref[...]).astype(_BF16)
    for c in range(D_FF // FF_CHUNK):
        lo = c * FF_CHUNK
        a = jnp.dot(h, w13_ref[:, lo:lo + FF_CHUNK], preferred_element_type=_F32)
        b = jnp.dot(h, w13_ref[:, D_FF + lo:D_FF + lo + FF_CHUNK], preferred_element_type=_F32)
        act = (jax.nn.silu(a) * b).astype(_BF16)
        part = jnp.dot(act, w2_ref[lo:lo + FF_CHUNK, :], preferred_element_type=_F32)
        if c == 0:
            acc_ref[...] = part
        else:
            acc_ref[...] += part
    y = x + 0.5 * acc_ref[...]
    if final_norm:
        y = _rms_rows(y, fg_ref[...])
    o_ref[...] = y


def _ffn(x, gain, w13, w2, final_gain, final_norm):
    n, d = x.shape
    assert n % TOKEN_TILE == 0 and D_FF % FF_CHUNK == 0
    row = pl.BlockSpec((TOKEN_TILE, d), lambda i: (i, 0))
    vec = pl.BlockSpec((1, d), lambda i: (0, 0))
    return pl.pallas_call(
        functools.partial(_ffn_kernel, final_norm=final_norm),
        grid=(n // TOKEN_TILE,),
        in_specs=[row, vec, _vmem_spec(), _vmem_spec(), vec],
        out_specs=row,
        out_shape=jax.ShapeDtypeStruct((n, d), _F32),
        scratch_shapes=[pltpu.VMEM((TOKEN_TILE, d), _F32)],
        compiler_params=_params("parallel"),
        name="ffn",
    )(x, gain, w13, w2, final_gain)


def _segment_sum(sq, seg):
    hi = sq.astype(_BF16)
    lo = (sq - hi.astype(_F32)).astype(_BF16)
    return (jnp.dot(hi, seg, preferred_element_type=_F32)
            + jnp.dot(lo, seg, preferred_element_type=_F32))


def _rope_lanes(x, cos, sin, first_half):
    partner = jnp.where(first_half, pltpu.roll(x, LANES - HEAD_DIM // 2, axis=1),
                        pltpu.roll(x, HEAD_DIM // 2, axis=1))
    return x * cos + partner * sin


def _inproj_kernel(x_ref, g_ref, w_ref, qg_ref, kg_ref, seg_ref, cos_ref, sin_ref,
                   q_ref, k_ref, v_ref, rq_ref, rk_ref, rv_ref, rg_ref):
    h = _rms_rows(x_ref[...], g_ref[...]).astype(_BF16)
    cos = cos_ref[...]
    sin = sin_ref[...]
    lane = lax.broadcasted_iota(jnp.int32, cos.shape, 1)
    first_half = (lane & (HEAD_DIM - 1)) < HEAD_DIM // 2
    seg = seg_ref[...]
    inv_hd = 1.0 / HEAD_DIM

    def head_norm_rope(xcols, gain, seg_block):
        ss = _segment_sum(xcols * xcols, seg_block)
        xn = xcols * lax.rsqrt(ss * inv_hd + NORM_EPS) * gain
        return [_rope_lanes(xn[:, j:j + LANES], cos, sin, first_half)
                for j in range(0, xn.shape[1], LANES)]

    offset = [0]

    def project(width):
        lo = offset[0]
        offset[0] += width
        return jnp.dot(h, w_ref[:, lo:lo + width], preferred_element_type=_F32)

    scale = HEAD_DIM ** -0.5 * math.log2(math.e)
    heads_per_block = LANES // HEAD_DIM
    aq = project(ATTN_Q_W)
    qcols = []
    for j in range(0, ATTN_Q_W, MXU_TILE):
        qcols += head_norm_rope(aq[:, j:j + MXU_TILE], qg_ref[:, j:j + MXU_TILE], seg)
    for j, col in enumerate(qcols):
        col_t = (col * scale).T.astype(_BF16)
        for s in range(heads_per_block):
            q_ref[heads_per_block * j + s] = col_t[s * HEAD_DIM:(s + 1) * HEAD_DIM]
    akv = project(2 * ATTN_KV_W)
    ak, av = akv[:, :ATTN_KV_W], akv[:, ATTN_KV_W:]
    kcols = head_norm_rope(ak, kg_ref[...], seg[:ATTN_KV_W, :ATTN_KV_W])
    for j, col in enumerate(kcols):
        col = col.astype(_BF16)
        for s in range(heads_per_block):
            k_ref[heads_per_block * j + s] = col[:, s * HEAD_DIM:(s + 1) * HEAD_DIM]
    av_t = av.T.astype(_BF16)
    ones = jnp.ones((v_ref.shape[2] - HEAD_DIM, VT_CHUNK), _BF16)
    for g in range(ATTN_KV_HEADS):
        for c in range(av_t.shape[1] // VT_CHUNK):
            v_ref[g, c, :HEAD_DIM] = av_t[g * HEAD_DIM:(g + 1) * HEAD_DIM,
                                          c * VT_CHUNK:(c + 1) * VT_CHUNK]
            v_ref[g, c, HEAD_DIM:] = ones

    rq = project(RET_QK_W)
    for j in range(0, RET_QK_W, LANES):
        rq_ref[:, j:j + LANES] = _rope_lanes(rq[:, j:j + LANES], cos, sin, first_half)
    rk = project(RET_QK_W)
    for j in range(0, RET_QK_W, LANES):
        rk_ref[:, j:j + LANES] = (_rope_lanes(rk[:, j:j + LANES], cos, sin, first_half)
                                  * (RET_KEY_DIM ** -0.5))
    rv_ref[...] = project(RET_V_W).astype(_BF16)
    rg_ref[...] = project(RET_V_W)


def _inproj(x, gain, w_in, q_gain, k_gain, seg, cos, sin, seq_len):
    n, d = x.shape
    tm = TOKEN_TILE
    assert n % tm == 0 and seq_len % tm == 0 and tm % VT_CHUNK == 0
    row = lambda w: pl.BlockSpec((tm, w), lambda i: (i, 0))
    vec = lambda w: pl.BlockSpec((1, w), lambda i: (0, 0))
    heads = lambda nh: pl.BlockSpec((nh, tm, HEAD_DIM), lambda i: (0, i, 0))
    heads_t = pl.BlockSpec((ATTN_HEADS, HEAD_DIM, tm), lambda i: (0, 0, i))
    vt_rows = _attn_tiling(seq_len)[2]
    chunks_t = pl.BlockSpec((ATTN_KV_HEADS, tm // VT_CHUNK, vt_rows, VT_CHUNK),
                            lambda i: (0, i, 0, 0))
    table = pl.BlockSpec((tm, LANES), lambda i: (i % (seq_len // tm), 0))
    return pl.pallas_call(
        _inproj_kernel,
        grid=(n // tm,),
        in_specs=[row(d), vec(d), _vmem_spec(), vec(ATTN_Q_W), vec(ATTN_KV_W), _vmem_spec(),
                  table, table],
        out_specs=[heads_t, heads(ATTN_KV_HEADS), chunks_t,
                   row(RET_QK_W), row(RET_QK_W), row(RET_V_W), row(RET_V_W)],
        out_shape=[jax.ShapeDtypeStruct((ATTN_HEADS, HEAD_DIM, n), _BF16),
                   jax.ShapeDtypeStruct((ATTN_KV_HEADS, n, HEAD_DIM), _BF16),
                   jax.ShapeDtypeStruct((ATTN_KV_HEADS, n // VT_CHUNK, vt_rows, VT_CHUNK), _BF16),
                   jax.ShapeDtypeStruct((n, RET_QK_W), _F32),
                   jax.ShapeDtypeStruct((n, RET_QK_W), _F32),
                   jax.ShapeDtypeStruct((n, RET_V_W), _BF16),
                   jax.ShapeDtypeStruct((n, RET_V_W), _F32)],
        compiler_params=_params("parallel"),
        name="inproj",
    )(x, gain, w_in, q_gain, k_gain, seg, cos, sin)


def _attn_kernel(qt_ref, k_ref, vt_ref, o_ref, acc_ref, s0_ref, s1_ref, s2_ref, s3_ref, *, seq_len,
                 split):
    qt = jnp.concatenate([qt_ref[h] for h in range(ATTN_GROUP)], axis=1)
    tq = qt_ref.shape[2]
    tk = s0_ref.shape[0]
    nk = seq_len // tk
    sub = tk // VT_CHUNK
    acc_ref[...] = jnp.zeros(acc_ref.shape, _F32)

    width = qt.shape[1] // split
    blocks = [slice(b * width, (b + 1) * width) for b in range(split)]

    def scores(j, dst_ref, m_prev, blk):
        start = pl.multiple_of(j * tk, tk)
        kj = k_ref[0, pl.ds(start, tk), :]
        st = jnp.dot(kj, qt[:, blk], preferred_element_type=_F32)
        dst_ref[:, blk] = st
        m_old = m_prev[:, blk]
        m_next = jnp.maximum(m_old, jnp.max(st, axis=0, keepdims=True))
        return m_next, jnp.exp2(m_old - m_next)

    def consume(j, src_ref, m, alpha, blk):
        pt = jnp.exp2(src_ref[:, blk] - m[:, blk])
        vt = jnp.concatenate([vt_ref[0, sub * j + u] for u in range(sub)], axis=1)
        pv = jnp.dot(vt, pt.astype(_BF16), preferred_element_type=_F32)
        acc_ref[:, blk] = alpha[:, blk] * acc_ref[:, blk] + pv

    def chunk_step(j_score, dst_ref, m_prev, j_use=None, src_ref=None, m=None, alpha=None):
        parts = []
        for blk in blocks:
            if j_score is not None:
                parts.append(scores(j_score, dst_ref, m_prev, blk))
            if j_use is not None:
                consume(j_use, src_ref, m, alpha, blk)
        if j_score is None:
            return None, None
        return (jnp.concatenate([p[0] for p in parts], axis=1),
                jnp.concatenate([p[1] for p in parts], axis=1))

    bufs = (s0_ref, s1_ref, s2_ref, s3_ref)
    m0, a0 = chunk_step(0, bufs[0], jnp.full((1, qt.shape[1]), -jnp.inf, _F32))
    m1, a1 = chunk_step(1, bufs[1], m0)

    def half_step(j, carry, cur, nxt, lookahead):
        m_a, alpha_a, m_b, alpha_b = carry
        m_c, alpha_c = chunk_step(j + 2 if lookahead else None, nxt[0], m_b, j, cur[0], m_a, alpha_a)
        m_d, alpha_d = chunk_step(j + 3 if lookahead else None, nxt[1], m_c, j + 1, cur[1], m_b,
                                  alpha_b)
        return (m_c, alpha_c, m_d, alpha_d) if lookahead else None

    unroll = ATTN_UNROLL if nk - 4 >= 2 * ATTN_UNROLL else 4

    def body(i, carry):
        j = unroll * i
        for h in range(0, unroll, 4):
            carry = half_step(j + h, carry, bufs[:2], bufs[2:], True)
            carry = half_step(j + h + 2, carry, bufs[2:], bufs[:2], True)
        return carry

    n_body = (nk - 4) // unroll
    carry = lax.fori_loop(0, n_body, body, (m0, a0, m1, a1))
    for j in range(n_body * unroll, nk - 4, 4):
        carry = half_step(j, carry, bufs[:2], bufs[2:], True)
        carry = half_step(j + 2, carry, bufs[2:], bufs[:2], True)
    carry = half_step(nk - 4, carry, bufs[:2], bufs[2:], True)
    half_step(nk - 2, carry, bufs[2:], bufs[:2], False)
    out_t = acc_ref[:HEAD_DIM, :] / acc_ref[HEAD_DIM:HEAD_DIM + 1, :]
    pair = LANES // HEAD_DIM
    for c in range(ATTN_GROUP // pair):
        stacked = jnp.concatenate(
            [out_t[:, (pair * c + s) * tq:(pair * c + s + 1) * tq] for s in range(pair)], axis=0)
        o_ref[:, c * LANES:(c + 1) * LANES] = stacked.T.astype(o_ref.dtype)


def _attention(qt, k, vt, batch, seq_len):
    n = k.shape[1]
    tq, tk, vt_rows = _attn_tiling(seq_len)
    assert n == batch * seq_len and seq_len % (4 * tk) == 0 and seq_len % tq == 0
    assert tk % VT_CHUNK == 0 and vt.shape[2] == vt_rows
    nq = seq_len // tq
    cols = ATTN_GROUP * tq
    return pl.pallas_call(
        functools.partial(_attn_kernel, seq_len=seq_len, split=cols // ATTN_LANE_BLOCK),
        grid=(batch, ATTN_KV_HEADS, nq),
        in_specs=[
            pl.BlockSpec((ATTN_GROUP, HEAD_DIM, tq), lambda b, g, i: (g, 0, b * nq + i)),
            pl.BlockSpec((1, seq_len, HEAD_DIM), lambda b, g, i: (g, b, 0)),
            pl.BlockSpec((1, seq_len // VT_CHUNK, vt_rows, VT_CHUNK), lambda b, g, i: (g, b, 0, 0)),
        ],
        out_specs=pl.BlockSpec((tq, ATTN_GROUP * HEAD_DIM), lambda b, g, i: (b * nq + i, g)),
        out_shape=jax.ShapeDtypeStruct((n, ATTN_Q_W), _BF16),
        scratch_shapes=[pltpu.VMEM((vt_rows, cols), _F32),
                        ] + [pltpu.VMEM((tk, cols), _F32)] * 4,
        compiler_params=_params("parallel", "parallel", "arbitrary"),
        name=f"attn_t{seq_len}",
    )(qt, k, vt)


def _ret_kernel(dec_ref, qf_ref, kf_ref, vf_ref, qb_ref, kb_ref, vb_ref, of_ref, ob_ref,
                dmat_ref, qd_ref, kd_ref, cd_ref, s_ref):
    c = RET_CHUNK
    t = pl.program_id(1)

    @pl.when(t == 0)
    def _init():
        row = lax.broadcasted_iota(jnp.int32, (c, c), 0).astype(_F32)
        col = lax.broadcasted_iota(jnp.int32, (c, c), 1).astype(_F32)
        idx = lax.broadcasted_iota(jnp.int32, (c, LANES), 0).astype(_F32)
        for slot in range(2 * RET_HEADS):
            fwd = slot < RET_HEADS
            lg_c = jnp.log1p(-jnp.exp(dec_ref[slot:slot + 1, :]))
            lg = lg_c[:, :LANES]
            diff = (row - col) if fwd else (col - row)
            mask = (diff >= 0) if fwd else (diff > 0)
            dmat_ref[slot] = jnp.where(mask, jnp.exp(jnp.where(mask, diff, 0.0) * lg_c), 0.0)
            qd_ref[slot] = jnp.exp(((idx + 1.0) if fwd else (c - idx)) * lg)
            kd_ref[slot] = jnp.exp(((c - 1.0 - idx) if fwd else idx) * lg)
            cd_ref[slot] = jnp.exp(c * lg) * jnp.ones((8, LANES), _F32)
        s_ref[...] = jnp.zeros(s_ref.shape, _F32)

    slots = range(2 * RET_HEADS)
    n_chunks = qf_ref.shape[0] // c
    items = [(ci, s) for ci in range(n_chunks) for s in slots]
    qs, ks, vs, rows = {}, {}, {}, {}
    for ci, s in items:
        fwd = s < RET_HEADS
        hd = s % RET_HEADS
        r0 = (ci if fwd else n_chunks - 1 - ci) * c
        key_cols = slice(hd * RET_KEY_DIM, (hd + 1) * RET_KEY_DIM)
        qs[ci, s] = (qf_ref if fwd else qb_ref)[r0:r0 + c, key_cols]
        ks[ci, s] = (kf_ref if fwd else kb_ref)[r0:r0 + c, key_cols]
        vs[ci, s] = (vf_ref if fwd else vb_ref)[r0:r0 + c,
                                                hd * RET_VALUE_DIM:(hd + 1) * RET_VALUE_DIM]
        rows[ci, s] = r0
    qk = {i: lax.dot_general(qs[i].astype(_BF16), ks[i].astype(_BF16), _NT,
                             preferred_element_type=_F32) for i in items}
    kv = {i: lax.dot_general((ks[i] * kd_ref[i[1]][:, :RET_KEY_DIM]).astype(_BF16), vs[i], _TN,
                             preferred_element_type=_F32) for i in items}
    states = {}
    for s in slots:
        state = s_ref[s]
        for ci in range(n_chunks):
            states[ci, s] = state
            state = state * cd_ref[s][:1, :] + kv[ci, s]
        s_ref[s] = state
    for i in items:
        s = i[1]
        lhs = jnp.concatenate([(qk[i] * dmat_ref[s]).astype(_BF16),
                               (qs[i] * qd_ref[s][:, :RET_KEY_DIM]).astype(_BF16)], axis=1)
        rhs = jnp.concatenate([vs[i], states[i].astype(_BF16)], axis=0)
        out_ref = of_ref if s < RET_HEADS else ob_ref
        hd = s % RET_HEADS
        out_ref[rows[i]:rows[i] + c, hd * RET_VALUE_DIM:(hd + 1) * RET_VALUE_DIM] = jnp.dot(
            lhs, rhs, preferred_element_type=_F32)


def _retention(rq, rk, rv, dec, batch, seq_len):
    n = rq.shape[0]
    c = RET_CHUNK
    blk = RET_BLOCK
    assert n == batch * seq_len and seq_len % blk == 0 and blk % c == 0 and RET_VALUE_DIM == LANES
    nc = seq_len // blk
    fwd = lambda b, t: b * nc + t
    bwd = lambda b, t: b * nc + (nc - 1 - t)
    qk = lambda f: pl.BlockSpec((blk, RET_QK_W), lambda b, t: (f(b, t), 0))
    vv = lambda f: pl.BlockSpec((blk, RET_V_W), lambda b, t: (f(b, t), 0))
    slots = 2 * RET_HEADS
    return pl.pallas_call(
        _ret_kernel,
        grid=(batch, nc),
        in_specs=[pl.BlockSpec((slots, c), lambda b, t: (0, 0)),
                  qk(fwd), qk(fwd), vv(fwd), qk(bwd), qk(bwd), vv(bwd)],
        out_specs=[vv(fwd), vv(bwd)],
        out_shape=[jax.ShapeDtypeStruct((n, RET_V_W), _F32)] * 2,
        scratch_shapes=[pltpu.VMEM((slots, c, c), _F32),
                        pltpu.VMEM((slots, c, LANES), _F32),
                        pltpu.VMEM((slots, c, LANES), _F32),
                        pltpu.VMEM((slots, 8, LANES), _F32),
                        pltpu.VMEM((slots, RET_KEY_DIM, RET_VALUE_DIM), _F32)],
        compiler_params=_params("arbitrary", "arbitrary"),
        name=f"retention_t{seq_len}",
    )(dec, rq, rk, rv, rq, rk, rv)


def _merge_kernel(x_ref, g_ref, ao_ref, rf_ref, rb_ref, rg_ref, rn_ref, wg_ref, bg_ref,
                  wba_ref, wbr_ref, wo_ref, o_ref):
    x = x_ref[...]
    h = _rms_rows(x, g_ref[...]).astype(_BF16)
    gates = jax.nn.sigmoid(jnp.dot(h, wg_ref[...], preferred_element_type=_F32) + bg_ref[...])
    ya = jnp.dot(ao_ref[...], wba_ref[...], preferred_element_type=_F32)
    y = rf_ref[...] + rb_ref[...]
    cols = []
    for j in range(0, RET_V_W, RET_VALUE_DIM):
        yh = y[:, j:j + RET_VALUE_DIM]
        mu = jnp.mean(yh, axis=-1, keepdims=True)
        yc = yh - mu
        var = jnp.mean(yc * yc, axis=-1, keepdims=True)
        cols.append(yc * lax.rsqrt(var + NORM_EPS))
    yn = jnp.concatenate(cols, axis=1) * rn_ref[...]
    yr_in = (jax.nn.silu(rg_ref[...]) * yn).astype(_BF16)
    yr = jnp.dot(yr_in, wbr_ref[...], preferred_element_type=_F32)
    mixed = (gates[:, :D_MODEL] * ya + gates[:, D_MODEL:] * yr).astype(_BF16)
    o_ref[...] = x + jnp.dot(mixed, wo_ref[...], preferred_element_type=_F32)


def _merge(x, gain, ao, rf, rb, rg, ret_gain, w_gate, b_gate, w_ba, w_br, w_out):
    n, d = x.shape
    tm = TOKEN_TILE
    row = lambda w: pl.BlockSpec((tm, w), lambda i: (i, 0))
    vec = lambda w: pl.BlockSpec((1, w), lambda i: (0, 0))
    return pl.pallas_call(
        _merge_kernel,
        grid=(n // tm,),
        in_specs=[row(d), vec(d), row(ATTN_Q_W), row(RET_V_W), row(RET_V_W), row(RET_V_W),
                  vec(RET_V_W), _vmem_spec(), vec(2 * d), _vmem_spec(), _vmem_spec(), _vmem_spec()],
        out_specs=row(d),
        out_shape=jax.ShapeDtypeStruct((n, d), _F32),
        compiler_params=_params("parallel"),
        name="merge",
    )(x, gain, ao, rf, rb, rg, ret_gain, w_gate, b_gate, w_ba, w_br, w_out)


def _trunk(xs, p):
    depth = p["ffn1_w13"].shape[0]
    shapes = [x.shape[:2] for x in xs]
    xs = [x.reshape(-1, D_MODEL) for x in xs]
    cos, sin = _rope_tables(max(t for _, t in shapes))

    seg_id = jnp.arange(MXU_TILE) // HEAD_DIM
    seg = (seg_id[:, None] == seg_id[None, :]).astype(_BF16)
    bf = lambda w: w.astype(_BF16)
    vec = lambda g: g.astype(_F32)[None, :]
    final_gain = vec(p["final_norm"])

    for l in range(depth):
        ffn1 = (vec(p["ffn1_norm"][l]), bf(p["ffn1_w13"][l]), bf(p["ffn1_w2"][l]), final_gain)
        ffn2 = (vec(p["ffn2_norm"][l]), bf(p["ffn2_w13"][l]), bf(p["ffn2_w2"][l]), final_gain)
        mix_gain = vec(p["mix_norm"][l])
        proj = (mix_gain, bf(p["w_in"][l]), vec(jnp.tile(p["q_norm"][l], ATTN_HEADS)),
                vec(jnp.tile(p["k_norm"][l], ATTN_KV_HEADS)), seg, cos, sin)
        dec = jnp.concatenate([p["ret_decay_fwd"][l], p["ret_decay_bwd"][l]]).astype(_F32)
        dec = jnp.broadcast_to(dec[:, None], (2 * RET_HEADS, RET_CHUNK))
        merge = (vec(p["ret_norm"][l]), bf(p["w_gate"][l]), vec(p["b_gate"][l]),
                 bf(p["w_branch_attn"][l]), bf(p["w_branch_ret"][l]), bf(p["w_out"][l]))
        for i, (b, t) in enumerate(shapes):
            x = _ffn(xs[i], *ffn1, False)
            q, k, v, rq, rk, rv, rg = _inproj(x, *proj, t)
            ao = _attention(q, k, v, b, t)
            rf, rb = _retention(rq, rk, rv, dec, b, t)
            x = _merge(x, mix_gain, ao, rf, rb, rg, *merge)
            xs[i] = _ffn(x, *ffn2, l == depth - 1)

    return [x.reshape(b, t, D_MODEL) for x, (b, t) in zip(xs, shapes)]


def kernel(x_prompt, x_sample, ffn1_norm, ffn1_w13, ffn1_w2, mix_norm, w_in, q_norm, k_norm,
           ret_decay_fwd, ret_decay_bwd, ret_norm, w_branch_attn, w_branch_ret, w_gate, b_gate,
           w_out, ffn2_norm, ffn2_w13, ffn2_w2, final_norm):
    p = dict(ffn1_norm=ffn1_norm, ffn1_w13=ffn1_w13, ffn1_w2=ffn1_w2, mix_norm=mix_norm, w_in=w_in,
             q_norm=q_norm, k_norm=k_norm, ret_decay_fwd=ret_decay_fwd, ret_decay_bwd=ret_decay_bwd,
             ret_norm=ret_norm, w_branch_attn=w_branch_attn, w_branch_ret=w_branch_ret, w_gate=w_gate,
             b_gate=b_gate, w_out=w_out, ffn2_norm=ffn2_norm, ffn2_w13=ffn2_w13, ffn2_w2=ffn2_w2,
             final_norm=final_norm)
    y_prompt, y_sample = _trunk([x_prompt, x_sample], p)
    return (y_prompt, y_sample)
```

```python
import functools
import math

import jax
import jax.numpy as jnp
from jax import lax
from jax.experimental import pallas as pl
from jax.experimental.pallas import tpu as pltpu

D_MODEL = 1024
GRID_W = 64
HEAD_DIM = 64
ATTN_HEADS = 8
ATTN_KV_HEADS = 2
ATTN_GROUP = ATTN_HEADS // ATTN_KV_HEADS
RET_HEADS = 4
RET_KEY_DIM = HEAD_DIM
RET_VALUE_DIM = 2 * RET_KEY_DIM
D_FF = 2816
ROPE_THETA = 10000.0
ROPE_AXIS_PAIRS = HEAD_DIM // 4
NORM_EPS = 1e-6
ATTN_Q_W = ATTN_HEADS * HEAD_DIM
ATTN_KV_W = ATTN_KV_HEADS * HEAD_DIM
RET_QK_W = RET_HEADS * RET_KEY_DIM
RET_V_W = RET_HEADS * RET_VALUE_DIM
IN_PROJ_W = ATTN_Q_W + 2 * ATTN_KV_W + 2 * RET_QK_W + 2 * RET_V_W

LANES = 128
MXU_TILE = 256
VMEM_LIMIT = 56 * 1024 * 1024

TOKEN_TILE = 512
FF_CHUNK = MXU_TILE
ATTN_LONG = (256, 512, HEAD_DIM + 64, 2)
ATTN_SHORT = (512, 256, HEAD_DIM + 16, 1)
VT_CHUNK = 256
ATTN_UNROLL = 8
ATTN_LANE_BLOCK = 512
RET_CHUNK = 128
RET_BLOCK = 512
ROPE_ROWS = 1024

_BF16 = jnp.bfloat16
_F32 = jnp.float32
_NT = (((1,), (1,)), ((), ()))
_TN = (((0,), (0,)), ((), ()))


def _vmem_spec():
    return pl.BlockSpec(memory_space=pltpu.VMEM)


def _params(*sem):
    return pltpu.CompilerParams(dimension_semantics=sem, vmem_limit_bytes=VMEM_LIMIT)


def _attn_tiling(seq_len):
    short = seq_len // ATTN_LONG[1] - 4 < 2 * ATTN_UNROLL
    return ATTN_SHORT if short else ATTN_LONG


def _rms_rows(x, gain):
    ms = jnp.mean(x * x, axis=-1, keepdims=True)
    return x * lax.rsqrt(ms + NORM_EPS) * gain


def _rope_table_kernel(freq_ref, cos_ref, sin_ref):
    rows = cos_ref.shape[0]
    t = pl.program_id(0) * rows + lax.broadcasted_iota(jnp.int32, (rows, LANES), 0)
    lane = lax.broadcasted_iota(jnp.int32, (rows, LANES), 1)
    pair = lane & (2 * ROPE_AXIS_PAIRS - 1)
    shift = GRID_W.bit_length() - 1
    pos = jnp.where(pair < ROPE_AXIS_PAIRS, t >> shift, t & (GRID_W - 1)).astype(_F32)
    ang = pos * freq_ref[...]
    sign = jnp.where((lane & (HEAD_DIM - 1)) < HEAD_DIM // 2, -1.0, 1.0).astype(_F32)
    cos_ref[...] = jnp.cos(ang)
    sin_ref[...] = jnp.sin(ang) * sign


def _rope_tables(t_max):
    assert GRID_W & (GRID_W - 1) == 0 and t_max % ROPE_ROWS == 0
    freqs = ROPE_THETA ** (-jnp.arange(ROPE_AXIS_PAIRS, dtype=_F32) / ROPE_AXIS_PAIRS)
    freq_lanes = jnp.tile(freqs, LANES // ROPE_AXIS_PAIRS)[None, :]
    return pl.pallas_call(
        _rope_table_kernel,
        grid=(t_max // ROPE_ROWS,),
        in_specs=[pl.BlockSpec((1, LANES), lambda i: (0, 0))],
        out_specs=[pl.BlockSpec((ROPE_ROWS, LANES), lambda i: (i, 0))] * 2,
        out_shape=[jax.ShapeDtypeStruct((t_max, LANES), _F32)] * 2,
        compiler_params=_params("parallel"),
        name="rope_tables",
    )(freq_lanes)


def _ffn_kernel(x_ref, g_ref, w13_ref, w2_ref, fg_ref, o_ref, acc_ref, *, final_norm):
    x = x_ref[...]
    h = _rms_rows(x, g_ref[...]).astype(_BF16)
    for c in range(D_FF // FF_CHUNK):
        lo = c * FF_CHUNK
        a = jnp.dot(h, w13_ref[:, lo:lo + FF_CHUNK], preferred_element_type=_F32)
        b = jnp.dot(h, w13_ref[:, D_FF + lo:D_FF + lo + FF_CHUNK], preferred_element_type=_F32)
        act = (jax.nn.silu(a) * b).astype(_BF16)
        part = jnp.dot(act, w2_ref[lo:lo + FF_CHUNK, :], preferred_element_type=_F32)
        if c == 0:
            acc_ref[...] = part
        else:
            acc_ref[...] += part
    y = x + 0.5 * acc_ref[...]
    if final_norm:
        y = _rms_rows(y, fg_ref[...])
    o_ref[...] = y


def _ffn(x, gain, w13, w2, final_gain, final_norm):
    n, d = x.shape
    assert n % TOKEN_TILE == 0 and D_FF % FF_CHUNK == 0
    row = pl.BlockSpec((TOKEN_TILE, d), lambda i: (i, 0))
    vec = pl.BlockSpec((1, d), lambda i: (0, 0))
    return pl.pallas_call(
        functools.partial(_ffn_kernel, final_norm=final_norm),
        grid=(n // TOKEN_TILE,),
        in_specs=[row, vec, _vmem_spec(), _vmem_spec(), vec],
        out_specs=row,
        out_shape=jax.ShapeDtypeStruct((n, d), _F32),
        scratch_shapes=[pltpu.VMEM((TOKEN_TILE, d), _F32)],
        compiler_params=_params("parallel"),
        name="ffn",
    )(x, gain, w13, w2, final_gain)


def _segment_sum(sq, seg):
    hi = sq.astype(_BF16)
    lo = (sq - hi.astype(_F32)).astype(_BF16)
    return (jnp.dot(hi, seg, preferred_element_type=_F32)
            + jnp.dot(lo, seg, preferred_element_type=_F32))


def _rope_lanes(x, cos, sin, first_half):
    partner = jnp.where(first_half, pltpu.roll(x, LANES - HEAD_DIM // 2, axis=1),
                        pltpu.roll(x, HEAD_DIM // 2, axis=1))
    return x * cos + partner * sin


def _inproj_kernel(x_ref, g_ref, w_ref, qg_ref, kg_ref, seg_ref, cos_ref, sin_ref,
                   q_ref, k_ref, v_ref, rq_ref, rk_ref, rv_ref, rg_ref):
    h = _rms_rows(x_ref[...], g_ref[...]).astype(_BF16)
    cos = cos_ref[...]
    sin = sin_ref[...]
    lane = lax.broadcasted_iota(jnp.int32, cos.shape, 1)
    first_half = (lane & (HEAD_DIM - 1)) < HEAD_DIM // 2
    seg = seg_ref[...]
    inv_hd = 1.0 / HEAD_DIM

    def head_norm_rope(xcols, gain, seg_block):
        ss = _segment_sum(xcols * xcols, seg_block)
        xn = xcols * lax.rsqrt(ss * inv_hd + NORM_EPS) * gain
        return [_rope_lanes(xn[:, j:j + LANES], cos, sin, first_half)
                for j in range(0, xn.shape[1], LANES)]

    offset = [0]

    def project(width):
        lo = offset[0]
        offset[0] += width
        return jnp.dot(h, w_ref[:, lo:lo + width], preferred_element_type=_F32)

    scale = HEAD_DIM ** -0.5 * math.log2(math.e)
    heads_per_block = LANES // HEAD_DIM
    aq = project(ATTN_Q_W)
    qcols = []
    for j in range(0, ATTN_Q_W, MXU_TILE):
        qcols += head_norm_rope(aq[:, j:j + MXU_TILE], qg_ref[:, j:j + MXU_TILE], seg)
    for j, col in enumerate(qcols):
        col_t = (col * scale).T.astype(_BF16)
        for s in range(heads_per_block):
            q_ref[heads_per_block * j + s] = col_t[s * HEAD_DIM:(s + 1) * HEAD_DIM]
    akv = project(2 * ATTN_KV_W)
    ak, av = akv[:, :ATTN_KV_W], akv[:, ATTN_KV_W:]
    kcols = head_norm_rope(ak, kg_ref[...], seg[:ATTN_KV_W, :ATTN_KV_W])
    for j, col in enumerate(kcols):
        col = col.astype(_BF16)
        for s in range(heads_per_block):
            k_ref[heads_per_block * j + s] = col[:, s * HEAD_DIM:(s + 1) * HEAD_DIM]
    av_t = av.T.astype(_BF16)
    ones = jnp.ones((v_ref.shape[2] - HEAD_DIM, VT_CHUNK), _BF16)
    for g in range(ATTN_KV_HEADS):
        for c in range(av_t.shape[1] // VT_CHUNK):
            v_ref[g, c, :HEAD_DIM] = av_t[g * HEAD_DIM:(g + 1) * HEAD_DIM,
                                          c * VT_CHUNK:(c + 1) * VT_CHUNK]
            v_ref[g, c, HEAD_DIM:] = ones

    rq = project(RET_QK_W)
    for j in range(0, RET_QK_W, LANES):
        rq_ref[:, j:j + LANES] = _rope_lanes(rq[:, j:j + LANES], cos, sin, first_half)
    rk = project(RET_QK_W)
    for j in range(0, RET_QK_W, LANES):
        rk_ref[:, j:j + LANES] = (_rope_lanes(rk[:, j:j + LANES], cos, sin, first_half)
                                  * (RET_KEY_DIM ** -0.5))
    rv_ref[...] = project(RET_V_W).astype(_BF16)
    rg_ref[...] = project(RET_V_W)


def _inproj(x, gain, w_in, q_gain, k_gain, seg, cos, sin, seq_len):
    n, d = x.shape
    tm = TOKEN_TILE
    assert n % tm == 0 and seq_len % tm == 0 and tm % VT_CHUNK == 0
    row = lambda w: pl.BlockSpec((tm, w), lambda i: (i, 0))
    vec = lambda w: pl.BlockSpec((1, w), lambda i: (0, 0))
    heads = lambda nh: pl.BlockSpec((nh, tm, HEAD_DIM), lambda i: (0, i, 0))
    heads_t = pl.BlockSpec((ATTN_HEADS, HEAD_DIM, tm), lambda i: (0, 0, i))
    vt_rows = _attn_tiling(seq_len)[2]
    chunks_t = pl.BlockSpec((ATTN_KV_HEADS, tm // VT_CHUNK, vt_rows, VT_CHUNK),
                            lambda i: (0, i, 0, 0))
    table = pl.BlockSpec((tm, LANES), lambda i: (i % (seq_len // tm), 0))
    return pl.pallas_call(
        _inproj_kernel,
        grid=(n // tm,),
        in_specs=[row(d), vec(d), _vmem_spec(), vec(ATTN_Q_W), vec(ATTN_KV_W), _vmem_spec(),
                  table, table],
        out_specs=[heads_t, heads(ATTN_KV_HEADS), chunks_t,
                   row(RET_QK_W), row(RET_QK_W), row(RET_V_W), row(RET_V_W)],
        out_shape=[jax.ShapeDtypeStruct((ATTN_HEADS, HEAD_DIM, n), _BF16),
                   jax.ShapeDtypeStruct((ATTN_KV_HEADS, n, HEAD_DIM), _BF16),
                   jax.ShapeDtypeStruct((ATTN_KV_HEADS, n // VT_CHUNK, vt_rows, VT_CHUNK), _BF16),
                   jax.ShapeDtypeStruct((n, RET_QK_W), _F32),
                   jax.ShapeDtypeStruct((n, RET_QK_W), _F32),
                   jax.ShapeDtypeStruct((n, RET_V_W), _BF16),
                   jax.ShapeDtypeStruct((n, RET_V_W), _F32)],
        compiler_params=_params("parallel"),
        name="inproj",
    )(x, gain, w_in, q_gain, k_gain, seg, cos, sin)


def _attn_kernel(qt_ref, k_ref, vt_ref, o_ref, acc_ref, s0_ref, s1_ref, s2_ref, s3_ref, *, seq_len,
                 split, ahead):
    qt = jnp.concatenate([qt_ref[h] for h in range(ATTN_GROUP)], axis=1)
    tq = qt_ref.shape[2]
    tk = s0_ref.shape[0]
    nk = seq_len // tk
    sub = tk // VT_CHUNK
    acc_ref[...] = jnp.zeros(acc_ref.shape, _F32)

    width = qt.shape[1] // split
    blocks = [slice(b * width, (b + 1) * width) for b in range(split)]

    def scores(j, dst_ref, m_prev, blk):
        start = pl.multiple_of(j * tk, tk)
        kj = k_ref[0, pl.ds(start, tk), :]
        st = jnp.dot(kj, qt[:, blk], preferred_element_type=_F32)
        dst_ref[:, blk] = st
        m_old = m_prev[:, blk]
        m_next = jnp.maximum(m_old, jnp.max(st, axis=0, keepdims=True))
        return m_next, jnp.exp2(m_old - m_next)

    def consume(j, src_ref, m, alpha, blk):
        pt = jnp.exp2(src_ref[:, blk] - m[:, blk])
        vt = jnp.concatenate([vt_ref[0, sub * j + u] for u in range(sub)], axis=1)
        pv = jnp.dot(vt, pt.astype(_BF16), preferred_element_type=_F32)
        acc_ref[:, blk] = alpha[:, blk] * acc_ref[:, blk] + pv

    def chunk_step(j_score, dst_ref, m_prev, j_use=None, src_ref=None, m=None, alpha=None):
        parts = []
        for blk in blocks:
            if j_score is not None:
                parts.append(scores(j_score, dst_ref, m_prev, blk))
            if j_use is not None:
                consume(j_use, src_ref, m, alpha, blk)
        if j_score is None:
            return None, None
        return (jnp.concatenate([p[0] for p in parts], axis=1),
                jnp.concatenate([p[1] for p in parts], axis=1))

    bufs = (s0_ref, s1_ref, s2_ref, s3_ref)
    buf = lambda j: bufs[j % len(bufs)]
    carry = ()
    m_run = jnp.full((1, qt.shape[1]), -jnp.inf, _F32)
    for c in range(ahead):
        m_run, alpha = chunk_step(c, buf(c), m_run)
        carry += ((m_run, alpha),)

    def step(j, slot, carry):
        scored = chunk_step(j + ahead, buf(slot + ahead), carry[-1][0], j, buf(slot), *carry[0])
        return carry[1:] + (scored,)

    n_steps = nk - ahead
    unroll = ATTN_UNROLL if n_steps >= 3 * ATTN_UNROLL else len(bufs)

    def body(i, carry):
        for u in range(unroll):
            carry = step(unroll * i + u, u, carry)
        return carry

    carry = lax.fori_loop(0, n_steps // unroll, body, carry)
    for j in range(n_steps // unroll * unroll, n_steps):
        carry = step(j, j, carry)
    for c, (m, alpha) in enumerate(carry):
        chunk_step(None, None, None, n_steps + c, buf(n_steps + c), m, alpha)
    out_t = acc_ref[:HEAD_DIM, :] / acc_ref[HEAD_DIM:HEAD_DIM + 1, :]
    pair = LANES // HEAD_DIM
    for c in range(ATTN_GROUP // pair):
        stacked = jnp.concatenate(
            [out_t[:, (pair * c + s) * tq:(pair * c + s + 1) * tq] for s in range(pair)], axis=0)
        o_ref[:, c * LANES:(c + 1) * LANES] = stacked.T.astype(o_ref.dtype)


def _attention(qt, k, vt, batch, seq_len):
    n = k.shape[1]
    tq, tk, vt_rows, ahead = _attn_tiling(seq_len)
    assert n == batch * seq_len and seq_len % (4 * tk) == 0 and seq_len % tq == 0
    assert tk % VT_CHUNK == 0 and vt.shape[2] == vt_rows
    nq = seq_len // tq
    cols = ATTN_GROUP * tq
    return pl.pallas_call(
        functools.partial(_attn_kernel, seq_len=seq_len, split=cols // ATTN_LANE_BLOCK,
                          ahead=ahead),
        grid=(batch, ATTN_KV_HEADS, nq),
        in_specs=[
            pl.BlockSpec((ATTN_GROUP, HEAD_DIM, tq), lambda b, g, i: (g, 0, b * nq + i)),
            pl.BlockSpec((1, seq_len, HEAD_DIM), lambda b, g, i: (g, b, 0)),
            pl.BlockSpec((1, seq_len // VT_CHUNK, vt_rows, VT_CHUNK), lambda b, g, i: (g, b, 0, 0)),
        ],
        out_specs=pl.BlockSpec((tq, ATTN_GROUP * HEAD_DIM), lambda b, g, i: (b * nq + i, g)),
        out_shape=jax.ShapeDtypeStruct((n, ATTN_Q_W), _BF16),
        scratch_shapes=[pltpu.VMEM((vt_rows, cols), _F32),
                        ] + [pltpu.VMEM((tk, cols), _F32)] * 4,
        compiler_params=_params("parallel", "parallel", "arbitrary"),
        name=f"attn_t{seq_len}",
    )(qt, k, vt)


def _ret_kernel(dec_ref, qf_ref, kf_ref, vf_ref, qb_ref, kb_ref, vb_ref, of_ref, ob_ref,
                dmat_ref, qd_ref, kd_ref, cd_ref, s_ref):
    c = RET_CHUNK
    t = pl.program_id(1)

    @pl.when(t == 0)
    def _init():
        row = lax.broadcasted_iota(jnp.int32, (c, c), 0).astype(_F32)
        col = lax.broadcasted_iota(jnp.int32, (c, c), 1).astype(_F32)
        idx = lax.broadcasted_iota(jnp.int32, (c, LANES), 0).astype(_F32)
        for slot in range(2 * RET_HEADS):
            fwd = slot < RET_HEADS
            lg_c = jnp.log1p(-jnp.exp(dec_ref[slot:slot + 1, :]))
            lg = lg_c[:, :LANES]
            diff = (row - col) if fwd else (col - row)
            mask = (diff >= 0) if fwd else (diff > 0)
            dmat_ref[slot] = jnp.where(mask, jnp.exp(jnp.where(mask, diff, 0.0) * lg_c), 0.0)
            qd_ref[slot] = jnp.exp(((idx + 1.0) if fwd else (c - idx)) * lg)
            kd_ref[slot] = jnp.exp(((c - 1.0 - idx) if fwd else idx) * lg)
            cd_ref[slot] = jnp.exp(c * lg) * jnp.ones((8, LANES), _F32)
        s_ref[...] = jnp.zeros(s_ref.shape, _F32)

    slots = range(2 * RET_HEADS)
    n_chunks = qf_ref.shape[0] // c
    items = [(ci, s) for ci in range(n_chunks) for s in slots]
    qs, ks, vs, rows = {}, {}, {}, {}
    for ci, s in items:
        fwd = s < RET_HEADS
        hd = s % RET_HEADS
        r0 = (ci if fwd else n_chunks - 1 - ci) * c
        key_cols = slice(hd * RET_KEY_DIM, (hd + 1) * RET_KEY_DIM)
        qs[ci, s] = (qf_ref if fwd else qb_ref)[r0:r0 + c, key_cols]
        ks[ci, s] = (kf_ref if fwd else kb_ref)[r0:r0 + c, key_cols]
        vs[ci, s] = (vf_ref if fwd else vb_ref)[r0:r0 + c,
                                                hd * RET_VALUE_DIM:(hd + 1) * RET_VALUE_DIM]
        rows[ci, s] = r0
    qk = {i: lax.dot_general(qs[i].astype(_BF16), ks[i].astype(_BF16), _NT,
                             preferred_element_type=_F32) for i in items}
    kv = {i: lax.dot_general((ks[i] * kd_ref[i[1]][:, :RET_KEY_DIM]).astype(_BF16), vs[i], _TN,
                             preferred_element_type=_F32) for i in items}
    states = {}
    for s in slots:
        state = s_ref[s]
        for ci in range(n_chunks):
            states[ci, s] = state
            state = state * cd_ref[s][:1, :] + kv[ci, s]
        s_ref[s] = state
    for i in items:
        s = i[1]
        lhs = jnp.concatenate([(qk[i] * dmat_ref[s]).astype(_BF16),
                               (qs[i] * qd_ref[s][:, :RET_KEY_DIM]).astype(_BF16)], axis=1)
        rhs = jnp.concatenate([vs[i], states[i].astype(_BF16)], axis=0)
        out_ref = of_ref if s < RET_HEADS else ob_ref
        hd = s % RET_HEADS
        out_ref[rows[i]:rows[i] + c, hd * RET_VALUE_DIM:(hd + 1) * RET_VALUE_DIM] = jnp.dot(
            lhs, rhs, preferred_element_type=_F32)


def _retention(rq, rk, rv, dec, batch, seq_len):
    n = rq.shape[0]
    c = RET_CHUNK
    blk = RET_BLOCK
    assert n == batch * seq_len and seq_len % blk == 0 and blk % c == 0 and RET_VALUE_DIM == LANES
    nc = seq_len // blk
    fwd = lambda b, t: b * nc + t
    bwd = lambda b, t: b * nc + (nc - 1 - t)
    qk = lambda f: pl.BlockSpec((blk, RET_QK_W), lambda b, t: (f(b, t), 0))
    vv = lambda f: pl.BlockSpec((blk, RET_V_W), lambda b, t: (f(b, t), 0))
    slots = 2 * RET_HEADS
    return pl.pallas_call(
        _ret_kernel,
        grid=(batch, nc),
        in_specs=[pl.BlockSpec((slots, c), lambda b, t: (0, 0)),
                  qk(fwd), qk(fwd), vv(fwd), qk(bwd), qk(bwd), vv(bwd)],
        out_specs=[vv(fwd), vv(bwd)],
        out_shape=[jax.ShapeDtypeStruct((n, RET_V_W), _F32)] * 2,
        scratch_shapes=[pltpu.VMEM((slots, c, c), _F32),
                        pltpu.VMEM((slots, c, LANES), _F32),
                        pltpu.VMEM((slots, c, LANES), _F32),
                        pltpu.VMEM((slots, 8, LANES), _F32),
                        pltpu.VMEM((slots, RET_KEY_DIM, RET_VALUE_DIM), _F32)],
        compiler_params=_params("arbitrary", "arbitrary"),
        name=f"retention_t{seq_len}",
    )(dec, rq, rk, rv, rq, rk, rv)


def _merge_kernel(x_ref, g_ref, ao_ref, rf_ref, rb_ref, rg_ref, rn_ref, wg_ref, bg_ref,
                  wba_ref, wbr_ref, wo_ref, o_ref):
    x = x_ref[...]
    h = _rms_rows(x, g_ref[...]).astype(_BF16)
    gates = jax.nn.sigmoid(jnp.dot(h, wg_ref[...], preferred_element_type=_F32) + bg_ref[...])
    ya = jnp.dot(ao_ref[...], wba_ref[...], preferred_element_type=_F32)
    y = rf_ref[...] + rb_ref[...]
    cols = []
    for j in range(0, RET_V_W, RET_VALUE_DIM):
        yh = y[:, j:j + RET_VALUE_DIM]
        mu = jnp.mean(yh, axis=-1, keepdims=True)
        yc = yh - mu
        var = jnp.mean(yc * yc, axis=-1, keepdims=True)
        cols.append(yc * lax.rsqrt(var + NORM_EPS))
    yn = jnp.concatenate(cols, axis=1) * rn_ref[...]
    yr_in = (jax.nn.silu(rg_ref[...]) * yn).astype(_BF16)
    yr = jnp.dot(yr_in, wbr_ref[...], preferred_element_type=_F32)
    mixed = (gates[:, :D_MODEL] * ya + gates[:, D_MODEL:] * yr).astype(_BF16)
    o_ref[...] = x + jnp.dot(mixed, wo_ref[...], preferred_element_type=_F32)


def _merge(x, gain, ao, rf, rb, rg, ret_gain, w_gate, b_gate, w_ba, w_br, w_out):
    n, d = x.shape
    tm = TOKEN_TILE
    row = lambda w: pl.BlockSpec((tm, w), lambda i: (i, 0))
    vec = lambda w: pl.BlockSpec((1, w), lambda i: (0, 0))
    return pl.pallas_call(
        _merge_kernel,
        grid=(n // tm,),
        in_specs=[row(d), vec(d), row(ATTN_Q_W), row(RET_V_W), row(RET_V_W), row(RET_V_W),
                  vec(RET_V_W), _vmem_spec(), vec(2 * d), _vmem_spec(), _vmem_spec(), _vmem_spec()],
        out_specs=row(d),
        out_shape=jax.ShapeDtypeStruct((n, d), _F32),
        compiler_params=_params("parallel"),
        name="merge",
    )(x, gain, ao, rf, rb, rg, ret_gain, w_gate, b_gate, w_ba, w_br, w_out)


def _trunk(xs, p):
    depth = p["ffn1_w13"].shape[0]
    shapes = [x.shape[:2] for x in xs]
    xs = [x.reshape(-1, D_MODEL) for x in xs]
    cos, sin = _rope_tables(max(t for _, t in shapes))

    seg_id = jnp.arange(MXU_TILE) // HEAD_DIM
    seg = (seg_id[:, None] == seg_id[None, :]).astype(_BF16)
    bf = lambda w: w.astype(_BF16)
    vec = lambda g: g.astype(_F32)[None, :]
    final_gain = vec(p["final_norm"])

    for l in range(depth):
        ffn1 = (vec(p["ffn1_norm"][l]), bf(p["ffn1_w13"][l]), bf(p["ffn1_w2"][l]), final_gain)
        ffn2 = (vec(p["ffn2_norm"][l]), bf(p["ffn2_w13"][l]), bf(p["ffn2_w2"][l]), final_gain)
        mix_gain = vec(p["mix_norm"][l])
        proj = (mix_gain, bf(p["w_in"][l]), vec(jnp.tile(p["q_norm"][l], ATTN_HEADS)),
                vec(jnp.tile(p["k_norm"][l], ATTN_KV_HEADS)), seg, cos, sin)
        dec = jnp.concatenate([p["ret_decay_fwd"][l], p["ret_decay_bwd"][l]]).astype(_F32)
        dec = jnp.broadcast_to(dec[:, None], (2 * RET_HEADS, RET_CHUNK))
        merge = (vec(p["ret_norm"][l]), bf(p["w_gate"][l]), vec(p["b_gate"][l]),
                 bf(p["w_branch_attn"][l]), bf(p["w_branch_ret"][l]), bf(p["w_out"][l]))
        for i, (b, t) in enumerate(shapes):
            x = _ffn(xs[i], *ffn1, False)
            q, k, v, rq, rk, rv, rg = _inproj(x, *proj, t)
            ao = _attention(q, k, v, b, t)
            rf, rb = _retention(rq, rk, rv, dec, b, t)
            x = _merge(x, mix_gain, ao, rf, rb, rg, *merge)
            xs[i] = _ffn(x, *ffn2, l == depth - 1)

    return [x.reshape(b, t, D_MODEL) for x, (b, t) in zip(xs, shapes)]


def kernel(x_prompt, x_sample, ffn1_norm, ffn1_w13, ffn1_w2, mix_norm, w_in, q_norm, k_norm,
           ret_decay_fwd, ret_decay_bwd, ret_norm, w_branch_attn, w_branch_ret, w_gate, b_gate,
           w_out, ffn2_norm, ffn2_w13, ffn2_w2, final_norm):
    p = dict(ffn1_norm=ffn1_norm, ffn1_w13=ffn1_w13, ffn1_w2=ffn1_w2, mix_norm=mix_norm, w_in=w_in,
             q_norm=q_norm, k_norm=k_norm, ret_decay_fwd=ret_decay_fwd, ret_decay_bwd=ret_decay_bwd,
             ret_norm=ret_norm, w_branch_attn=w_branch_attn, w_branch_ret=w_branch_ret, w_gate=w_gate,
             b_gate=b_gate, w_out=w_out, ffn2_norm=ffn2_norm, ffn2_w13=ffn2_w13, ffn2_w2=ffn2_w2,
             final_norm=final_norm)
    y_prompt, y_sample = _trunk([x_prompt, x_sample], p)
    return (y_prompt, y_sample)
```

```python
import functools
import math

import jax
import jax.numpy as jnp
from jax import lax
from jax.experimental import pallas as pl
from jax.experimental.pallas import tpu as pltpu

D_MODEL = 1024
GRID_W = 64
HEAD_DIM = 64
ATTN_HEADS = 8
ATTN_KV_HEADS = 2
ATTN_GROUP = ATTN_HEADS // ATTN_KV_HEADS
RET_HEADS = 4
RET_KEY_DIM = HEAD_DIM
RET_VALUE_DIM = 2 * RET_KEY_DIM
D_FF = 2816
ROPE_THETA = 10000.0
ROPE_AXIS_PAIRS = HEAD_DIM // 4
NORM_EPS = 1e-6
ATTN_Q_W = ATTN_HEADS * HEAD_DIM
ATTN_KV_W = ATTN_KV_HEADS * HEAD_DIM
RET_QK_W = RET_HEADS * RET_KEY_DIM
RET_V_W = RET_HEADS * RET_VALUE_DIM
IN_PROJ_W = ATTN_Q_W + 2 * ATTN_KV_W + 2 * RET_QK_W + 2 * RET_V_W

LANES = 128
MXU_TILE = 256
VMEM_LIMIT = 56 * 1024 * 1024

TOKEN_TILE = 512
FF_CHUNK = MXU_TILE
ATTN_LONG = (512, 512, HEAD_DIM + 48, 2)
ATTN_SHORT = (512, 256, HEAD_DIM + 16, 1)
VT_CHUNK = 256
ATTN_UNROLL = 8
ATTN_LANE_BLOCK = 512
RET_CHUNK = 128
RET_BLOCK = 512
ROPE_ROWS = 1024

_BF16 = jnp.bfloat16
_F32 = jnp.float32
_NT = (((1,), (1,)), ((), ()))
_TN = (((0,), (0,)), ((), ()))


def _vmem_spec():
    return pl.BlockSpec(memory_space=pltpu.VMEM)


def _params(*sem):
    return pltpu.CompilerParams(dimension_semantics=sem, vmem_limit_bytes=VMEM_LIMIT)


def _attn_tiling(seq_len):
    short = seq_len // ATTN_LONG[1] - 4 < 2 * ATTN_UNROLL
    return ATTN_SHORT if short else ATTN_LONG


def _rms_rows(x, gain):
    ms = jnp.mean(x * x, axis=-1, keepdims=True)
    return x * lax.rsqrt(ms + NORM_EPS) * gain


def _rope_table_kernel(freq_ref, cos_ref, sin_ref):
    rows = cos_ref.shape[0]
    t = pl.program_id(0) * rows + lax.broadcasted_iota(jnp.int32, (rows, LANES), 0)
    lane = lax.broadcasted_iota(jnp.int32, (rows, LANES), 1)
    pair = lane & (2 * ROPE_AXIS_PAIRS - 1)
    shift = GRID_W.bit_length() - 1
    pos = jnp.where(pair < ROPE_AXIS_PAIRS, t >> shift, t & (GRID_W - 1)).astype(_F32)
    ang = pos * freq_ref[...]
    sign = jnp.where((lane & (HEAD_DIM - 1)) < HEAD_DIM // 2, -1.0, 1.0).astype(_F32)
    cos_ref[...] = jnp.cos(ang)
    sin_ref[...] = jnp.sin(ang) * sign


def _rope_tables(t_max):
    assert GRID_W & (GRID_W - 1) == 0 and t_max % ROPE_ROWS == 0
    freqs = ROPE_THETA ** (-jnp.arange(ROPE_AXIS_PAIRS, dtype=_F32) / ROPE_AXIS_PAIRS)
    freq_lanes = jnp.tile(freqs, LANES // ROPE_AXIS_PAIRS)[None, :]
    return pl.pallas_call(
        _rope_table_kernel,
        grid=(t_max // ROPE_ROWS,),
        in_specs=[pl.BlockSpec((1, LANES), lambda i: (0, 0))],
        out_specs=[pl.BlockSpec((ROPE_ROWS, LANES), lambda i: (i, 0))] * 2,
        out_shape=[jax.ShapeDtypeStruct((t_max, LANES), _F32)] * 2,
        compiler_params=_params("parallel"),
        name="rope_tables",
    )(freq_lanes)


def _ffn_kernel(x_ref, g_ref, w13_ref, w2_ref, fg_ref, o_ref, acc_ref, *, final_norm):
    x = x_ref[...]
    h = _rms_rows(x, g_ref[...]).astype(_BF16)
    for c in range(D_FF // FF_CHUNK):
        lo = c * FF_CHUNK
        a = jnp.dot(h, w13_ref[:, lo:lo + FF_CHUNK], preferred_element_type=_F32)
        b = jnp.dot(h, w13_ref[:, D_FF + lo:D_FF + lo + FF_CHUNK], preferred_element_type=_F32)
        act = (jax.nn.silu(a) * b).astype(_BF16)
        part = jnp.dot(act, w2_ref[lo:lo + FF_CHUNK, :], preferred_element_type=_F32)
        if c == 0:
            acc_ref[...] = part
        else:
            acc_ref[...] += part
    y = x + 0.5 * acc_ref[...]
    if final_norm:
        y = _rms_rows(y, fg_ref[...])
    o_ref[...] = y


def _ffn(x, gain, w13, w2, final_gain, final_norm):
    n, d = x.shape
    assert n % TOKEN_TILE == 0 and D_FF % FF_CHUNK == 0
    row = pl.BlockSpec((TOKEN_TILE, d), lambda i: (i, 0))
    vec = pl.BlockSpec((1, d), lambda i: (0, 0))
    return pl.pallas_call(
        functools.partial(_ffn_kernel, final_norm=final_norm),
        grid=(n // TOKEN_TILE,),
        in_specs=[row, vec, _vmem_spec(), _vmem_spec(), vec],
        out_specs=row,
        out_shape=jax.ShapeDtypeStruct((n, d), _F32),
        scratch_shapes=[pltpu.VMEM((TOKEN_TILE, d), _F32)],
        compiler_params=_params("parallel"),
        name="ffn",
    )(x, gain, w13, w2, final_gain)


def _segment_sum(sq, seg):
    hi = sq.astype(_BF16)
    lo = (sq - hi.astype(_F32)).astype(_BF16)
    return (jnp.dot(hi, seg, preferred_element_type=_F32)
            + jnp.dot(lo, seg, preferred_element_type=_F32))


def _rope_lanes(x, cos, sin, first_half):
    partner = jnp.where(first_half, pltpu.roll(x, LANES - HEAD_DIM // 2, axis=1),
                        pltpu.roll(x, HEAD_DIM // 2, axis=1))
    return x * cos + partner * sin


def _inproj_kernel(x_ref, g_ref, w_ref, qg_ref, kg_ref, seg_ref, cos_ref, sin_ref,
                   q_ref, k_ref, v_ref, rq_ref, rk_ref, rv_ref, rg_ref):
    h = _rms_rows(x_ref[...], g_ref[...]).astype(_BF16)
    cos = cos_ref[...]
    sin = sin_ref[...]
    lane = lax.broadcasted_iota(jnp.int32, cos.shape, 1)
    first_half = (lane & (HEAD_DIM - 1)) < HEAD_DIM // 2
    seg = seg_ref[...]
    inv_hd = 1.0 / HEAD_DIM

    def head_norm_rope(xcols, gain, seg_block):
        ss = _segment_sum(xcols * xcols, seg_block)
        xn = xcols * lax.rsqrt(ss * inv_hd + NORM_EPS) * gain
        return [_rope_lanes(xn[:, j:j + LANES], cos, sin, first_half)
                for j in range(0, xn.shape[1], LANES)]

    offset = [0]

    def project(width):
        lo = offset[0]
        offset[0] += width
        return jnp.dot(h, w_ref[:, lo:lo + width], preferred_element_type=_F32)

    scale = HEAD_DIM ** -0.5 * math.log2(math.e)
    heads_per_block = LANES // HEAD_DIM
    aq = project(ATTN_Q_W)
    qcols = []
    for j in range(0, ATTN_Q_W, MXU_TILE):
        qcols += head_norm_rope(aq[:, j:j + MXU_TILE], qg_ref[:, j:j + MXU_TILE], seg)
    for j, col in enumerate(qcols):
        col_t = (col * scale).T.astype(_BF16)
        for s in range(heads_per_block):
            q_ref[heads_per_block * j + s] = col_t[s * HEAD_DIM:(s + 1) * HEAD_DIM]
    akv = project(2 * ATTN_KV_W)
    ak, av = akv[:, :ATTN_KV_W], akv[:, ATTN_KV_W:]
    kcols = head_norm_rope(ak, kg_ref[...], seg[:ATTN_KV_W, :ATTN_KV_W])
    for j, col in enumerate(kcols):
        col = col.astype(_BF16)
        for s in range(heads_per_block):
            k_ref[heads_per_block * j + s] = col[:, s * HEAD_DIM:(s + 1) * HEAD_DIM]
    av_t = av.T.astype(_BF16)
    ones = jnp.ones((v_ref.shape[2] - HEAD_DIM, VT_CHUNK), _BF16)
    for g in range(ATTN_KV_HEADS):
        for c in range(av_t.shape[1] // VT_CHUNK):
            v_ref[g, c, :HEAD_DIM] = av_t[g * HEAD_DIM:(g + 1) * HEAD_DIM,
                                          c * VT_CHUNK:(c + 1) * VT_CHUNK]
            v_ref[g, c, HEAD_DIM:] = ones

    rq = project(RET_QK_W)
    for j in range(0, RET_QK_W, LANES):
        rq_ref[:, j:j + LANES] = _rope_lanes(rq[:, j:j + LANES], cos, sin, first_half)
    rk = project(RET_QK_W)
    for j in range(0, RET_QK_W, LANES):
        rk_ref[:, j:j + LANES] = (_rope_lanes(rk[:, j:j + LANES], cos, sin, first_half)
                                  * (RET_KEY_DIM ** -0.5))
    rv_ref[...] = project(RET_V_W).astype(_BF16)
    rg_ref[...] = project(RET_V_W)


def _inproj(x, gain, w_in, q_gain, k_gain, seg, cos, sin, seq_len):
    n, d = x.shape
    tm = TOKEN_TILE
    assert n % tm == 0 and seq_len % tm == 0 and tm % VT_CHUNK == 0
    row = lambda w: pl.BlockSpec((tm, w), lambda i: (i, 0))
    vec = lambda w: pl.BlockSpec((1, w), lambda i: (0, 0))
    heads = lambda nh: pl.BlockSpec((nh, tm, HEAD_DIM), lambda i: (0, i, 0))
    heads_t = pl.BlockSpec((ATTN_HEADS, HEAD_DIM, tm), lambda i: (0, 0, i))
    vt_rows = _attn_tiling(seq_len)[2]
    chunks_t = pl.BlockSpec((ATTN_KV_HEADS, tm // VT_CHUNK, vt_rows, VT_CHUNK),
                            lambda i: (0, i, 0, 0))
    table = pl.BlockSpec((tm, LANES), lambda i: (i % (seq_len // tm), 0))
    return pl.pallas_call(
        _inproj_kernel,
        grid=(n // tm,),
        in_specs=[row(d), vec(d), _vmem_spec(), vec(ATTN_Q_W), vec(ATTN_KV_W), _vmem_spec(),
                  table, table],
        out_specs=[heads_t, heads(ATTN_KV_HEADS), chunks_t,
                   row(RET_QK_W), row(RET_QK_W), row(RET_V_W), row(RET_V_W)],
        out_shape=[jax.ShapeDtypeStruct((ATTN_HEADS, HEAD_DIM, n), _BF16),
                   jax.ShapeDtypeStruct((ATTN_KV_HEADS, n, HEAD_DIM), _BF16),
                   jax.ShapeDtypeStruct((ATTN_KV_HEADS, n // VT_CHUNK, vt_rows, VT_CHUNK), _BF16),
                   jax.ShapeDtypeStruct((n, RET_QK_W), _F32),
                   jax.ShapeDtypeStruct((n, RET_QK_W), _F32),
                   jax.ShapeDtypeStruct((n, RET_V_W), _BF16),
                   jax.ShapeDtypeStruct((n, RET_V_W), _F32)],
        compiler_params=_params("parallel"),
        name="inproj",
    )(x, gain, w_in, q_gain, k_gain, seg, cos, sin)


def _attn_kernel(qt_ref, k_ref, vt_ref, o_ref, acc_ref, s0_ref, s1_ref, s2_ref, s3_ref, *, seq_len,
                 split, ahead):
    qt = jnp.concatenate([qt_ref[h] for h in range(ATTN_GROUP)], axis=1)
    tq = qt_ref.shape[2]
    tk = s0_ref.shape[0]
    nk = seq_len // tk
    sub = tk // VT_CHUNK
    acc_ref[...] = jnp.zeros(acc_ref.shape, _F32)

    width = qt.shape[1] // split
    blocks = [slice(b * width, (b + 1) * width) for b in range(split)]

    def scores(j, dst_ref, m_prev, blk):
        start = pl.multiple_of(j * tk, tk)
        kj = k_ref[0, pl.ds(start, tk), :]
        st = jnp.dot(kj, qt[:, blk], preferred_element_type=_F32)
        dst_ref[:, blk] = st
        m_old = m_prev[:, blk]
        m_next = jnp.maximum(m_old, jnp.max(st, axis=0, keepdims=True))
        return m_next, jnp.exp2(m_old - m_next)

    def consume(j, src_ref, m, alpha, blk):
        pt = jnp.exp2(src_ref[:, blk] - m[:, blk])
        vt = jnp.concatenate([vt_ref[0, sub * j + u] for u in range(sub)], axis=1)
        pv = jnp.dot(vt, pt.astype(_BF16), preferred_element_type=_F32)
        acc_ref[:, blk] = alpha[:, blk] * acc_ref[:, blk] + pv

    def chunk_step(j_score, dst_ref, m_prev, j_use=None, src_ref=None, m=None, alpha=None):
        parts = []
        for blk in blocks:
            if j_score is not None:
                parts.append(scores(j_score, dst_ref, m_prev, blk))
            if j_use is not None:
                consume(j_use, src_ref, m, alpha, blk)
        if j_score is None:
            return None, None
        return (jnp.concatenate([p[0] for p in parts], axis=1),
                jnp.concatenate([p[1] for p in parts], axis=1))

    bufs = (s0_ref, s1_ref, s2_ref, s3_ref)
    buf = lambda j: bufs[j % len(bufs)]
    carry = ()
    m_run = jnp.full((1, qt.shape[1]), -jnp.inf, _F32)
    for c in range(ahead):
        m_run, alpha = chunk_step(c, buf(c), m_run)
        carry += ((m_run, alpha),)

    def step(j, slot, carry):
        scored = chunk_step(j + ahead, buf(slot + ahead), carry[-1][0], j, buf(slot), *carry[0])
        return carry[1:] + (scored,)

    n_steps = nk - ahead
    unroll = ATTN_UNROLL if n_steps >= 3 * ATTN_UNROLL else len(bufs)

    def body(i, carry):
        for u in range(unroll):
            carry = step(unroll * i + u, u, carry)
        return carry

    carry = lax.fori_loop(0, n_steps // unroll, body, carry)
    for j in range(n_steps // unroll * unroll, n_steps):
        carry = step(j, j, carry)
    for c, (m, alpha) in enumerate(carry):
        chunk_step(None, None, None, n_steps + c, buf(n_steps + c), m, alpha)
    out_t = acc_ref[:HEAD_DIM, :] / acc_ref[HEAD_DIM:HEAD_DIM + 1, :]
    pair = LANES // HEAD_DIM
    for c in range(ATTN_GROUP // pair):
        stacked = jnp.concatenate(
            [out_t[:, (pair * c + s) * tq:(pair * c + s + 1) * tq] for s in range(pair)], axis=0)
        o_ref[:, c * LANES:(c + 1) * LANES] = stacked.T.astype(o_ref.dtype)


def _attention(qt, k, vt, batch, seq_len):
    n = k.shape[1]
    tq, tk, vt_rows, ahead = _attn_tiling(seq_len)
    assert n == batch * seq_len and seq_len % (4 * tk) == 0 and seq_len % tq == 0
    assert tk % VT_CHUNK == 0 and vt.shape[2] == vt_rows
    nq = seq_len // tq
    cols = ATTN_GROUP * tq
    return pl.pallas_call(
        functools.partial(_attn_kernel, seq_len=seq_len, split=cols // ATTN_LANE_BLOCK,
                          ahead=ahead),
        grid=(batch, ATTN_KV_HEADS, nq),
        in_specs=[
            pl.BlockSpec((ATTN_GROUP, HEAD_DIM, tq), lambda b, g, i: (g, 0, b * nq + i)),
            pl.BlockSpec((1, seq_len, HEAD_DIM), lambda b, g, i: (g, b, 0)),
            pl.BlockSpec((1, seq_len // VT_CHUNK, vt_rows, VT_CHUNK), lambda b, g, i: (g, b, 0, 0)),
        ],
        out_specs=pl.BlockSpec((tq, ATTN_GROUP * HEAD_DIM), lambda b, g, i: (b * nq + i, g)),
        out_shape=jax.ShapeDtypeStruct((n, ATTN_Q_W), _BF16),
        scratch_shapes=[pltpu.VMEM((vt_rows, cols), _F32),
                        ] + [pltpu.VMEM((tk, cols), _F32)] * 4,
        compiler_params=_params("parallel", "parallel", "arbitrary"),
        name=f"attn_t{seq_len}",
    )(qt, k, vt)


def _ret_kernel(dec_ref, qf_ref, kf_ref, vf_ref, qb_ref, kb_ref, vb_ref, of_ref, ob_ref,
                dmat_ref, qd_ref, kd_ref, cd_ref, s_ref):
    c = RET_CHUNK
    t = pl.program_id(1)

    @pl.when(t == 0)
    def _init():
        row = lax.broadcasted_iota(jnp.int32, (c, c), 0).astype(_F32)
        col = lax.broadcasted_iota(jnp.int32, (c, c), 1).astype(_F32)
        idx = lax.broadcasted_iota(jnp.int32, (c, LANES), 0).astype(_F32)
        for slot in range(2 * RET_HEADS):
            fwd = slot < RET_HEADS
            lg_c = jnp.log1p(-jnp.exp(dec_ref[slot:slot + 1, :]))
            lg = lg_c[:, :LANES]
            diff = (row - col) if fwd else (col - row)
            mask = (diff >= 0) if fwd else (diff > 0)
            dmat_ref[slot] = jnp.where(mask, jnp.exp(jnp.where(mask, diff, 0.0) * lg_c), 0.0)
            qd_ref[slot] = jnp.exp(((idx + 1.0) if fwd else (c - idx)) * lg)
            kd_ref[slot] = jnp.exp(((c - 1.0 - idx) if fwd else idx) * lg)
            cd_ref[slot] = jnp.exp(c * lg) * jnp.ones((8, LANES), _F32)
        s_ref[...] = jnp.zeros(s_ref.shape, _F32)

    slots = range(2 * RET_HEADS)
    n_chunks = qf_ref.shape[0] // c
    items = [(ci, s) for ci in range(n_chunks) for s in slots]
    qs, ks, vs, rows = {}, {}, {}, {}
    for ci, s in items:
        fwd = s < RET_HEADS
        hd = s % RET_HEADS
        r0 = (ci if fwd else n_chunks - 1 - ci) * c
        key_cols = slice(hd * RET_KEY_DIM, (hd + 1) * RET_KEY_DIM)
        qs[ci, s] = (qf_ref if fwd else qb_ref)[r0:r0 + c, key_cols]
        ks[ci, s] = (kf_ref if fwd else kb_ref)[r0:r0 + c, key_cols]
        vs[ci, s] = (vf_ref if fwd else vb_ref)[r0:r0 + c,
                                                hd * RET_VALUE_DIM:(hd + 1) * RET_VALUE_DIM]
        rows[ci, s] = r0
    qk = {i: lax.dot_general(qs[i].astype(_BF16), ks[i].astype(_BF16), _NT,
                             preferred_element_type=_F32) for i in items}
    kv = {i: lax.dot_general((ks[i] * kd_ref[i[1]][:, :RET_KEY_DIM]).astype(_BF16), vs[i], _TN,
                             preferred_element_type=_F32) for i in items}
    states = {}
    for s in slots:
        state = s_ref[s]
        for ci in range(n_chunks):
            states[ci, s] = state
            state = state * cd_ref[s][:1, :] + kv[ci, s]
        s_ref[s] = state
    for i in items:
        s = i[1]
        lhs = jnp.concatenate([(qk[i] * dmat_ref[s]).astype(_BF16),
                               (qs[i] * qd_ref[s][:, :RET_KEY_DIM]).astype(_BF16)], axis=1)
        rhs = jnp.concatenate([vs[i], states[i].astype(_BF16)], axis=0)
        out_ref = of_ref if s < RET_HEADS else ob_ref
        hd = s % RET_HEADS
        out_ref[rows[i]:rows[i] + c, hd * RET_VALUE_DIM:(hd + 1) * RET_VALUE_DIM] = jnp.dot(
            lhs, rhs, preferred_element_type=_F32)


def _retention(rq, rk, rv, dec, batch, seq_len):
    n = rq.shape[0]
    c = RET_CHUNK
    blk = RET_BLOCK
    assert n == batch * seq_len and seq_len % blk == 0 and blk % c == 0 and RET_VALUE_DIM == LANES
    nc = seq_len // blk
    fwd = lambda b, t: b * nc + t
    bwd = lambda b, t: b * nc + (nc - 1 - t)
    qk = lambda f: pl.BlockSpec((blk, RET_QK_W), lambda b, t: (f(b, t), 0))
    vv = lambda f: pl.BlockSpec((blk, RET_V_W), lambda b, t: (f(b, t), 0))
    slots = 2 * RET_HEADS
    return pl.pallas_call(
        _ret_kernel,
        grid=(batch, nc),
        in_specs=[pl.BlockSpec((slots, c), lambda b, t: (0, 0)),
                  qk(fwd), qk(fwd), vv(fwd), qk(bwd), qk(bwd), vv(bwd)],
        out_specs=[vv(fwd), vv(bwd)],
        out_shape=[jax.ShapeDtypeStruct((n, RET_V_W), _F32)] * 2,
        scratch_shapes=[pltpu.VMEM((slots, c, c), _F32),
                        pltpu.VMEM((slots, c, LANES), _F32),
                        pltpu.VMEM((slots, c, LANES), _F32),
                        pltpu.VMEM((slots, 8, LANES), _F32),
                        pltpu.VMEM((slots, RET_KEY_DIM, RET_VALUE_DIM), _F32)],
        compiler_params=_params("arbitrary", "arbitrary"),
        name=f"retention_t{seq_len}",
    )(dec, rq, rk, rv, rq, rk, rv)


def _merge_kernel(x_ref, g_ref, ao_ref, rf_ref, rb_ref, rg_ref, rn_ref, wg_ref, bg_ref,
                  wba_ref, wbr_ref, wo_ref, o_ref):
    x = x_ref[...]
    h = _rms_rows(x, g_ref[...]).astype(_BF16)
    gates = jax.nn.sigmoid(jnp.dot(h, wg_ref[...], preferred_element_type=_F32) + bg_ref[...])
    ya = jnp.dot(ao_ref[...], wba_ref[...], preferred_element_type=_F32)
    y = rf_ref[...] + rb_ref[...]
    cols = []
    for j in range(0, RET_V_W, RET_VALUE_DIM):
        yh = y[:, j:j + RET_VALUE_DIM]
        mu = jnp.mean(yh, axis=-1, keepdims=True)
        yc = yh - mu
        var = jnp.mean(yc * yc, axis=-1, keepdims=True)
        cols.append(yc * lax.rsqrt(var + NORM_EPS))
    yn = jnp.concatenate(cols, axis=1) * rn_ref[...]
    yr_in = (jax.nn.silu(rg_ref[...]) * yn).astype(_BF16)
    yr = jnp.dot(yr_in, wbr_ref[...], preferred_element_type=_F32)
    mixed = (gates[:, :D_MODEL] * ya + gates[:, D_MODEL:] * yr).astype(_BF16)
    o_ref[...] = x + jnp.dot(mixed, wo_ref[...], preferred_element_type=_F32)


def _merge(x, gain, ao, rf, rb, rg, ret_gain, w_gate, b_gate, w_ba, w_br, w_out):
    n, d = x.shape
    tm = TOKEN_TILE
    row = lambda w: pl.BlockSpec((tm, w), lambda i: (i, 0))
    vec = lambda w: pl.BlockSpec((1, w), lambda i: (0, 0))
    return pl.pallas_call(
        _merge_kernel,
        grid=(n // tm,),
        in_specs=[row(d), vec(d), row(ATTN_Q_W), row(RET_V_W), row(RET_V_W), row(RET_V_W),
                  vec(RET_V_W), _vmem_spec(), vec(2 * d), _vmem_spec(), _vmem_spec(), _vmem_spec()],
        out_specs=row(d),
        out_shape=jax.ShapeDtypeStruct((n, d), _F32),
        compiler_params=_params("parallel"),
        name="merge",
    )(x, gain, ao, rf, rb, rg, ret_gain, w_gate, b_gate, w_ba, w_br, w_out)


def _trunk(xs, p):
    depth = p["ffn1_w13"].shape[0]
    shapes = [x.shape[:2] for x in xs]
    xs = [x.reshape(-1, D_MODEL) for x in xs]
    cos, sin = _rope_tables(max(t for _, t in shapes))

    seg_id = jnp.arange(MXU_TILE) // HEAD_DIM
    seg = (seg_id[:, None] == seg_id[None, :]).astype(_BF16)
    bf = lambda w: w.astype(_BF16)
    vec = lambda g: g.astype(_F32)[None, :]
    final_gain = vec(p["final_norm"])

    for l in range(depth):
        ffn1 = (vec(p["ffn1_norm"][l]), bf(p["ffn1_w13"][l]), bf(p["ffn1_w2"][l]), final_gain)
        ffn2 = (vec(p["ffn2_norm"][l]), bf(p["ffn2_w13"][l]), bf(p["ffn2_w2"][l]), final_gain)
        mix_gain = vec(p["mix_norm"][l])
        proj = (mix_gain, bf(p["w_in"][l]), vec(jnp.tile(p["q_norm"][l], ATTN_HEADS)),
                vec(jnp.tile(p["k_norm"][l], ATTN_KV_HEADS)), seg, cos, sin)
        dec = jnp.concatenate([p["ret_decay_fwd"][l], p["ret_decay_bwd"][l]]).astype(_F32)
        dec = jnp.broadcast_to(dec[:, None], (2 * RET_HEADS, RET_CHUNK))
        merge = (vec(p["ret_norm"][l]), bf(p["w_gate"][l]), vec(p["b_gate"][l]),
                 bf(p["w_branch_attn"][l]), bf(p["w_branch_ret"][l]), bf(p["w_out"][l]))
        for i, (b, t) in enumerate(shapes):
            x = _ffn(xs[i], *ffn1, False)
            q, k, v, rq, rk, rv, rg = _inproj(x, *proj, t)
            ao = _attention(q, k, v, b, t)
            rf, rb = _retention(rq, rk, rv, dec, b, t)
            x = _merge(x, mix_gain, ao, rf, rb, rg, *merge)
            xs[i] = _ffn(x, *ffn2, l == depth - 1)

    return [x.reshape(b, t, D_MODEL) for x, (b, t) in zip(xs, shapes)]


def kernel(x_prompt, x_sample, ffn1_norm, ffn1_w13, ffn1_w2, mix_norm, w_in, q_norm, k_norm,
           ret_decay_fwd, ret_decay_bwd, ret_norm, w_branch_attn, w_branch_ret, w_gate, b_gate,
           w_out, ffn2_norm, ffn2_w13, ffn2_w2, final_norm):
    p = dict(ffn1_norm=ffn1_norm, ffn1_w13=ffn1_w13, ffn1_w2=ffn1_w2, mix_norm=mix_norm, w_in=w_in,
             q_norm=q_norm, k_norm=k_norm, ret_decay_fwd=ret_decay_fwd, ret_decay_bwd=ret_decay_bwd,
             ret_norm=ret_norm, w_branch_attn=w_branch_attn, w_branch_ret=w_branch_ret, w_gate=w_gate,
             b_gate=b_gate, w_out=w_out, ffn2_norm=ffn2_norm, ffn2_w13=ffn2_w13, ffn2_w2=ffn2_w2,
             final_norm=final_norm)
    y_prompt, y_sample = _trunk([x_prompt, x_sample], p)
    return (y_prompt, y_sample)
```

```python
import functools
import math

import jax
import jax.numpy as jnp
from jax import lax
from jax.experimental import pallas as pl
from jax.experimental.pallas import tpu as pltpu

D_MODEL = 1024
GRID_W = 64
HEAD_DIM = 64
ATTN_HEADS = 8
ATTN_KV_HEADS = 2
ATTN_GROUP = ATTN_HEADS // ATTN_KV_HEADS
RET_HEADS = 4
RET_KEY_DIM = HEAD_DIM
RET_VALUE_DIM = 2 * RET_KEY_DIM
D_FF = 2816
ROPE_THETA = 10000.0
ROPE_AXIS_PAIRS = HEAD_DIM // 4
NORM_EPS = 1e-6
ATTN_Q_W = ATTN_HEADS * HEAD_DIM
ATTN_KV_W = ATTN_KV_HEADS * HEAD_DIM
RET_QK_W = RET_HEADS * RET_KEY_DIM
RET_V_W = RET_HEADS * RET_VALUE_DIM
IN_PROJ_W = ATTN_Q_W + 2 * ATTN_KV_W + 2 * RET_QK_W + 2 * RET_V_W

LANES = 128
MXU_TILE = 256
VMEM_LIMIT = 56 * 1024 * 1024

TOKEN_TILE = 1024
FF_CHUNK = MXU_TILE
ATTN_LONG = (512, 512, HEAD_DIM + 48, 1)
ATTN_SHORT = (512, 256, HEAD_DIM + 16, 1)
VT_CHUNK = 256
ATTN_UNROLL = 8
ATTN_LANE_BLOCK = 512
RET_CHUNK = 128
RET_BLOCK = 512
ROPE_ROWS = 1024

_BF16 = jnp.bfloat16
_F32 = jnp.float32
_NT = (((1,), (1,)), ((), ()))
_TN = (((0,), (0,)), ((), ()))


def _vmem_spec():
    return pl.BlockSpec(memory_space=pltpu.VMEM)


def _params(*sem):
    return pltpu.CompilerParams(dimension_semantics=sem, vmem_limit_bytes=VMEM_LIMIT)


def _attn_tiling(seq_len):
    short = seq_len // ATTN_LONG[1] - 4 < 2 * ATTN_UNROLL
    return ATTN_SHORT if short else ATTN_LONG


def _rms_rows(x, gain):
    ms = jnp.mean(x * x, axis=-1, keepdims=True)
    return x * lax.rsqrt(ms + NORM_EPS) * gain


def _rope_table_kernel(freq_ref, cos_ref, sin_ref):
    rows = cos_ref.shape[0]
    t = pl.program_id(0) * rows + lax.broadcasted_iota(jnp.int32, (rows, LANES), 0)
    lane = lax.broadcasted_iota(jnp.int32, (rows, LANES), 1)
    pair = lane & (2 * ROPE_AXIS_PAIRS - 1)
    shift = GRID_W.bit_length() - 1
    pos = jnp.where(pair < ROPE_AXIS_PAIRS, t >> shift, t & (GRID_W - 1)).astype(_F32)
    ang = pos * freq_ref[...]
    sign = jnp.where((lane & (HEAD_DIM - 1)) < HEAD_DIM // 2, -1.0, 1.0).astype(_F32)
    cos_ref[...] = jnp.cos(ang)
    sin_ref[...] = jnp.sin(ang) * sign


def _rope_tables(t_max):
    assert GRID_W & (GRID_W - 1) == 0 and t_max % ROPE_ROWS == 0
    freqs = ROPE_THETA ** (-jnp.arange(ROPE_AXIS_PAIRS, dtype=_F32) / ROPE_AXIS_PAIRS)
    freq_lanes = jnp.tile(freqs, LANES // ROPE_AXIS_PAIRS)[None, :]
    return pl.pallas_call(
        _rope_table_kernel,
        grid=(t_max // ROPE_ROWS,),
        in_specs=[pl.BlockSpec((1, LANES), lambda i: (0, 0))],
        out_specs=[pl.BlockSpec((ROPE_ROWS, LANES), lambda i: (i, 0))] * 2,
        out_shape=[jax.ShapeDtypeStruct((t_max, LANES), _F32)] * 2,
        compiler_params=_params("parallel"),
        name="rope_tables",
    )(freq_lanes)


def _ffn_kernel(x_ref, g_ref, w13_ref, w2_ref, fg_ref, o_ref, acc_ref, *, final_norm):
    x = x_ref[...]
    h = _rms_rows(x, g_ref[...]).astype(_BF16)
    for c in range(D_FF // FF_CHUNK):
        lo = c * FF_CHUNK
        a = jnp.dot(h, w13_ref[:, lo:lo + FF_CHUNK], preferred_element_type=_F32)
        b = jnp.dot(h, w13_ref[:, D_FF + lo:D_FF + lo + FF_CHUNK], preferred_element_type=_F32)
        act = (jax.nn.silu(a) * b).astype(_BF16)
        part = jnp.dot(act, w2_ref[lo:lo + FF_CHUNK, :], preferred_element_type=_F32)
        if c == 0:
            acc_ref[...] = part
        else:
            acc_ref[...] += part
    y = x + 0.5 * acc_ref[...]
    if final_norm:
        y = _rms_rows(y, fg_ref[...])
    o_ref[...] = y


def _ffn(x, gain, w13, w2, final_gain, final_norm):
    n, d = x.shape
    assert n % TOKEN_TILE == 0 and D_FF % FF_CHUNK == 0
    row = pl.BlockSpec((TOKEN_TILE, d), lambda i: (i, 0))
    vec = pl.BlockSpec((1, d), lambda i: (0, 0))
    return pl.pallas_call(
        functools.partial(_ffn_kernel, final_norm=final_norm),
        grid=(n // TOKEN_TILE,),
        in_specs=[row, vec, _vmem_spec(), _vmem_spec(), vec],
        out_specs=row,
        out_shape=jax.ShapeDtypeStruct((n, d), _F32),
        scratch_shapes=[pltpu.VMEM((TOKEN_TILE, d), _F32)],
        compiler_params=_params("parallel"),
        name="ffn",
    )(x, gain, w13, w2, final_gain)


def _segment_sum(sq, seg):
    hi = sq.astype(_BF16)
    lo = (sq - hi.astype(_F32)).astype(_BF16)
    return (jnp.dot(hi, seg, preferred_element_type=_F32)
            + jnp.dot(lo, seg, preferred_element_type=_F32))


def _rope_lanes(x, cos, sin, first_half):
    partner = jnp.where(first_half, pltpu.roll(x, LANES - HEAD_DIM // 2, axis=1),
                        pltpu.roll(x, HEAD_DIM // 2, axis=1))
    return x * cos + partner * sin


def _inproj_kernel(x_ref, g_ref, w_ref, qg_ref, kg_ref, seg_ref, cos_ref, sin_ref,
                   q_ref, k_ref, v_ref, rq_ref, rk_ref, rv_ref, rg_ref):
    h = _rms_rows(x_ref[...], g_ref[...]).astype(_BF16)
    cos = cos_ref[...]
    sin = sin_ref[...]
    lane = lax.broadcasted_iota(jnp.int32, cos.shape, 1)
    first_half = (lane & (HEAD_DIM - 1)) < HEAD_DIM // 2
    seg = seg_ref[...]
    inv_hd = 1.0 / HEAD_DIM

    def head_norm_rope(xcols, gain, seg_block):
        ss = _segment_sum(xcols * xcols, seg_block)
        xn = xcols * lax.rsqrt(ss * inv_hd + NORM_EPS) * gain
        return [_rope_lanes(xn[:, j:j + LANES], cos, sin, first_half)
                for j in range(0, xn.shape[1], LANES)]

    offset = [0]

    def project(width):
        lo = offset[0]
        offset[0] += width
        return jnp.dot(h, w_ref[:, lo:lo + width], preferred_element_type=_F32)

    scale = HEAD_DIM ** -0.5 * math.log2(math.e)
    heads_per_block = LANES // HEAD_DIM
    aq = project(ATTN_Q_W)
    qcols = []
    for j in range(0, ATTN_Q_W, MXU_TILE):
        qcols += head_norm_rope(aq[:, j:j + MXU_TILE], qg_ref[:, j:j + MXU_TILE], seg)
    for j, col in enumerate(qcols):
        col_t = (col * scale).T.astype(_BF16)
        for s in range(heads_per_block):
            q_ref[heads_per_block * j + s] = col_t[s * HEAD_DIM:(s + 1) * HEAD_DIM]
    akv = project(2 * ATTN_KV_W)
    ak, av = akv[:, :ATTN_KV_W], akv[:, ATTN_KV_W:]
    kcols = head_norm_rope(ak, kg_ref[...], seg[:ATTN_KV_W, :ATTN_KV_W])
    for j, col in enumerate(kcols):
        col = col.astype(_BF16)
        for s in range(heads_per_block):
            k_ref[heads_per_block * j + s] = col[:, s * HEAD_DIM:(s + 1) * HEAD_DIM]
    av_t = av.T.astype(_BF16)
    ones = jnp.ones((v_ref.shape[2] - HEAD_DIM, VT_CHUNK), _BF16)
    for g in range(ATTN_KV_HEADS):
        for c in range(av_t.shape[1] // VT_CHUNK):
            v_ref[g, c, :HEAD_DIM] = av_t[g * HEAD_DIM:(g + 1) * HEAD_DIM,
                                          c * VT_CHUNK:(c + 1) * VT_CHUNK]
            v_ref[g, c, HEAD_DIM:] = ones

    rq = project(RET_QK_W)
    for j in range(0, RET_QK_W, LANES):
        rq_ref[:, j:j + LANES] = _rope_lanes(rq[:, j:j + LANES], cos, sin, first_half)
    rk = project(RET_QK_W)
    for j in range(0, RET_QK_W, LANES):
        rk_ref[:, j:j + LANES] = (_rope_lanes(rk[:, j:j + LANES], cos, sin, first_half)
                                  * (RET_KEY_DIM ** -0.5))
    rv_ref[...] = project(RET_V_W).astype(_BF16)
    rg_ref[...] = project(RET_V_W)


def _inproj(x, gain, w_in, q_gain, k_gain, seg, cos, sin, seq_len):
    n, d = x.shape
    tm = TOKEN_TILE
    assert n % tm == 0 and seq_len % tm == 0 and tm % VT_CHUNK == 0
    row = lambda w: pl.BlockSpec((tm, w), lambda i: (i, 0))
    vec = lambda w: pl.BlockSpec((1, w), lambda i: (0, 0))
    heads = lambda nh: pl.BlockSpec((nh, tm, HEAD_DIM), lambda i: (0, i, 0))
    heads_t = pl.BlockSpec((ATTN_HEADS, HEAD_DIM, tm), lambda i: (0, 0, i))
    vt_rows = _attn_tiling(seq_len)[2]
    chunks_t = pl.BlockSpec((ATTN_KV_HEADS, tm // VT_CHUNK, vt_rows, VT_CHUNK),
                            lambda i: (0, i, 0, 0))
    table = pl.BlockSpec((tm, LANES), lambda i: (i % (seq_len // tm), 0))
    return pl.pallas_call(
        _inproj_kernel,
        grid=(n // tm,),
        in_specs=[row(d), vec(d), _vmem_spec(), vec(ATTN_Q_W), vec(ATTN_KV_W), _vmem_spec(),
                  table, table],
        out_specs=[heads_t, heads(ATTN_KV_HEADS), chunks_t,
                   row(RET_QK_W), row(RET_QK_W), row(RET_V_W), row(RET_V_W)],
        out_shape=[jax.ShapeDtypeStruct((ATTN_HEADS, HEAD_DIM, n), _BF16),
                   jax.ShapeDtypeStruct((ATTN_KV_HEADS, n, HEAD_DIM), _BF16),
                   jax.ShapeDtypeStruct((ATTN_KV_HEADS, n // VT_CHUNK, vt_rows, VT_CHUNK), _BF16),
                   jax.ShapeDtypeStruct((n, RET_QK_W), _F32),
                   jax.ShapeDtypeStruct((n, RET_QK_W), _F32),
                   jax.ShapeDtypeStruct((n, RET_V_W), _BF16),
                   jax.ShapeDtypeStruct((n, RET_V_W), _F32)],
        compiler_params=_params("parallel"),
        name="inproj",
    )(x, gain, w_in, q_gain, k_gain, seg, cos, sin)


def _attn_kernel(qt_ref, k_ref, vt_ref, o_ref, acc_ref, s0_ref, s1_ref, s2_ref, s3_ref, *, seq_len,
                 split, ahead):
    qt = jnp.concatenate([qt_ref[h] for h in range(ATTN_GROUP)], axis=1)
    tq = qt_ref.shape[2]
    tk = s0_ref.shape[0]
    nk = seq_len // tk
    sub = tk // VT_CHUNK
    acc_ref[...] = jnp.zeros(acc_ref.shape, _F32)

    width = qt.shape[1] // split
    blocks = [slice(b * width, (b + 1) * width) for b in range(split)]

    def scores(j, dst_ref, m_prev, blk):
        start = pl.multiple_of(j * tk, tk)
        kj = k_ref[0, pl.ds(start, tk), :]
        st = jnp.dot(kj, qt[:, blk], preferred_element_type=_F32)
        dst_ref[:, blk] = st
        m_old = m_prev[:, blk]
        m_next = jnp.maximum(m_old, jnp.max(st, axis=0, keepdims=True))
        return m_next, jnp.exp2(m_old - m_next)

    def consume(j, src_ref, m, alpha, blk):
        pt = jnp.exp2(src_ref[:, blk] - m[:, blk])
        vt = jnp.concatenate([vt_ref[0, sub * j + u] for u in range(sub)], axis=1)
        pv = jnp.dot(vt, pt.astype(_BF16), preferred_element_type=_F32)
        acc_ref[:, blk] = alpha[:, blk] * acc_ref[:, blk] + pv

    def chunk_step(j_score, dst_ref, m_prev, j_use=None, src_ref=None, m=None, alpha=None):
        parts = []
        for blk in blocks:
            if j_score is not None:
                parts.append(scores(j_score, dst_ref, m_prev, blk))
            if j_use is not None:
                consume(j_use, src_ref, m, alpha, blk)
        if j_score is None:
            return None, None
        return (jnp.concatenate([p[0] for p in parts], axis=1),
                jnp.concatenate([p[1] for p in parts], axis=1))

    bufs = (s0_ref, s1_ref, s2_ref, s3_ref)
    buf = lambda j: bufs[j % len(bufs)]
    carry = ()
    m_run = jnp.full((1, qt.shape[1]), -jnp.inf, _F32)
    for c in range(ahead):
        m_run, alpha = chunk_step(c, buf(c), m_run)
        carry += ((m_run, alpha),)

    def step(j, slot, carry):
        scored = chunk_step(j + ahead, buf(slot + ahead), carry[-1][0], j, buf(slot), *carry[0])
        return carry[1:] + (scored,)

    n_steps = nk - ahead
    unroll = ATTN_UNROLL if n_steps >= 3 * ATTN_UNROLL else len(bufs)

    def body(i, carry):
        for u in range(unroll):
            carry = step(unroll * i + u, u, carry)
        return carry

    carry = lax.fori_loop(0, n_steps // unroll, body, carry)
    for j in range(n_steps // unroll * unroll, n_steps):
        carry = step(j, j, carry)
    for c, (m, alpha) in enumerate(carry):
        chunk_step(None, None, None, n_steps + c, buf(n_steps + c), m, alpha)
    out_t = acc_ref[:HEAD_DIM, :] / acc_ref[HEAD_DIM:HEAD_DIM + 1, :]
    pair = LANES // HEAD_DIM
    for c in range(ATTN_GROUP // pair):
        stacked = jnp.concatenate(
            [out_t[:, (pair * c + s) * tq:(pair * c + s + 1) * tq] for s in range(pair)], axis=0)
        o_ref[:, c * LANES:(c + 1) * LANES] = stacked.T.astype(o_ref.dtype)


def _attention(qt, k, vt, batch, seq_len):
    n = k.shape[1]
    tq, tk, vt_rows, ahead = _attn_tiling(seq_len)
    assert n == batch * seq_len and seq_len % (4 * tk) == 0 and seq_len % tq == 0
    assert tk % VT_CHUNK == 0 and vt.shape[2] == vt_rows
    nq = seq_len // tq
    cols = ATTN_GROUP * tq
    return pl.pallas_call(
        functools.partial(_attn_kernel, seq_len=seq_len, split=cols // ATTN_LANE_BLOCK,
                          ahead=ahead),
        grid=(batch, ATTN_KV_HEADS, nq),
        in_specs=[
            pl.BlockSpec((ATTN_GROUP, HEAD_DIM, tq), lambda b, g, i: (g, 0, b * nq + i)),
            pl.BlockSpec((1, seq_len, HEAD_DIM), lambda b, g, i: (g, b, 0)),
            pl.BlockSpec((1, seq_len // VT_CHUNK, vt_rows, VT_CHUNK), lambda b, g, i: (g, b, 0, 0)),
        ],
        out_specs=pl.BlockSpec((tq, ATTN_GROUP * HEAD_DIM), lambda b, g, i: (b * nq + i, g)),
        out_shape=jax.ShapeDtypeStruct((n, ATTN_Q_W), _BF16),
        scratch_shapes=[pltpu.VMEM((vt_rows, cols), _F32),
                        ] + [pltpu.VMEM((tk, cols), _F32)] * 4,
        compiler_params=_params("parallel", "parallel", "arbitrary"),
        name=f"attn_t{seq_len}",
    )(qt, k, vt)


def _ret_kernel(dec_ref, qf_ref, kf_ref, vf_ref, qb_ref, kb_ref, vb_ref, of_ref, ob_ref,
                dmat_ref, qd_ref, kd_ref, cd_ref, s_ref):
    c = RET_CHUNK
    t = pl.program_id(1)

    @pl.when(t == 0)
    def _init():
        row = lax.broadcasted_iota(jnp.int32, (c, c), 0).astype(_F32)
        col = lax.broadcasted_iota(jnp.int32, (c, c), 1).astype(_F32)
        idx = lax.broadcasted_iota(jnp.int32, (c, LANES), 0).astype(_F32)
        for slot in range(2 * RET_HEADS):
            fwd = slot < RET_HEADS
            lg_c = jnp.log1p(-jnp.exp(dec_ref[slot:slot + 1, :]))
            lg = lg_c[:, :LANES]
            diff = (row - col) if fwd else (col - row)
            mask = (diff >= 0) if fwd else (diff > 0)
            dmat_ref[slot] = jnp.where(mask, jnp.exp(jnp.where(mask, diff, 0.0) * lg_c), 0.0)
            qd_ref[slot] = jnp.exp(((idx + 1.0) if fwd else (c - idx)) * lg)
            kd_ref[slot] = jnp.exp(((c - 1.0 - idx) if fwd else idx) * lg)
            cd_ref[slot] = jnp.exp(c * lg) * jnp.ones((8, LANES), _F32)
        s_ref[...] = jnp.zeros(s_ref.shape, _F32)

    slots = range(2 * RET_HEADS)
    n_chunks = qf_ref.shape[0] // c
    items = [(ci, s) for ci in range(n_chunks) for s in slots]
    qs, ks, vs, rows = {}, {}, {}, {}
    for ci, s in items:
        fwd = s < RET_HEADS
        hd = s % RET_HEADS
        r0 = (ci if fwd else n_chunks - 1 - ci) * c
        key_cols = slice(hd * RET_KEY_DIM, (hd + 1) * RET_KEY_DIM)
        qs[ci, s] = (qf_ref if fwd else qb_ref)[r0:r0 + c, key_cols]
        ks[ci, s] = (kf_ref if fwd else kb_ref)[r0:r0 + c, key_cols]
        vs[ci, s] = (vf_ref if fwd else vb_ref)[r0:r0 + c,
                                                hd * RET_VALUE_DIM:(hd + 1) * RET_VALUE_DIM]
        rows[ci, s] = r0
    qk = {i: lax.dot_general(qs[i].astype(_BF16), ks[i].astype(_BF16), _NT,
                             preferred_element_type=_F32) for i in items}
    kv = {i: lax.dot_general((ks[i] * kd_ref[i[1]][:, :RET_KEY_DIM]).astype(_BF16), vs[i], _TN,
                             preferred_element_type=_F32) for i in items}
    states = {}
    for s in slots:
        state = s_ref[s]
        for ci in range(n_chunks):
            states[ci, s] = state
            state = state * cd_ref[s][:1, :] + kv[ci, s]
        s_ref[s] = state
    for i in items:
        s = i[1]
        lhs = jnp.concatenate([(qk[i] * dmat_ref[s]).astype(_BF16),
                               (qs[i] * qd_ref[s][:, :RET_KEY_DIM]).astype(_BF16)], axis=1)
        rhs = jnp.concatenate([vs[i], states[i].astype(_BF16)], axis=0)
        out_ref = of_ref if s < RET_HEADS else ob_ref
        hd = s % RET_HEADS
        out_ref[rows[i]:rows[i] + c, hd * RET_VALUE_DIM:(hd + 1) * RET_VALUE_DIM] = jnp.dot(
            lhs, rhs, preferred_element_type=_F32)


def _retention(rq, rk, rv, dec, batch, seq_len):
    n = rq.shape[0]
    c = RET_CHUNK
    blk = RET_BLOCK
    assert n == batch * seq_len and seq_len % blk == 0 and blk % c == 0 and RET_VALUE_DIM == LANES
    nc = seq_len // blk
    fwd = lambda b, t: b * nc + t
    bwd = lambda b, t: b * nc + (nc - 1 - t)
    qk = lambda f: pl.BlockSpec((blk, RET_QK_W), lambda b, t: (f(b, t), 0))
    vv = lambda f: pl.BlockSpec((blk, RET_V_W), lambda b, t: (f(b, t), 0))
    slots = 2 * RET_HEADS
    return pl.pallas_call(
        _ret_kernel,
        grid=(batch, nc),
        in_specs=[pl.BlockSpec((slots, c), lambda b, t: (0, 0)),
                  qk(fwd), qk(fwd), vv(fwd), qk(bwd), qk(bwd), vv(bwd)],
        out_specs=[vv(fwd), vv(bwd)],
        out_shape=[jax.ShapeDtypeStruct((n, RET_V_W), _F32)] * 2,
        scratch_shapes=[pltpu.VMEM((slots, c, c), _F32),
                        pltpu.VMEM((slots, c, LANES), _F32),
                        pltpu.VMEM((slots, c, LANES), _F32),
                        pltpu.VMEM((slots, 8, LANES), _F32),
                        pltpu.VMEM((slots, RET_KEY_DIM, RET_VALUE_DIM), _F32)],
        compiler_params=_params("arbitrary", "arbitrary"),
        name=f"retention_t{seq_len}",
    )(dec, rq, rk, rv, rq, rk, rv)


def _merge_kernel(x_ref, g_ref, ao_ref, rf_ref, rb_ref, rg_ref, rn_ref, wg_ref, bg_ref,
                  wba_ref, wbr_ref, wo_ref, o_ref):
    x = x_ref[...]
    h = _rms_rows(x, g_ref[...]).astype(_BF16)
    gates = jax.nn.sigmoid(jnp.dot(h, wg_ref[...], preferred_element_type=_F32) + bg_ref[...])
    ya = jnp.dot(ao_ref[...], wba_ref[...], preferred_element_type=_F32)
    y = rf_ref[...] + rb_ref[...]
    cols = []
    for j in range(0, RET_V_W, RET_VALUE_DIM):
        yh = y[:, j:j + RET_VALUE_DIM]
        mu = jnp.mean(yh, axis=-1, keepdims=True)
        yc = yh - mu
        var = jnp.mean(yc * yc, axis=-1, keepdims=True)
        cols.append(yc * lax.rsqrt(var + NORM_EPS))
    yn = jnp.concatenate(cols, axis=1) * rn_ref[...]
    yr_in = (jax.nn.silu(rg_ref[...]) * yn).astype(_BF16)
    yr = jnp.dot(yr_in, wbr_ref[...], preferred_element_type=_F32)
    mixed = (gates[:, :D_MODEL] * ya + gates[:, D_MODEL:] * yr).astype(_BF16)
    o_ref[...] = x + jnp.dot(mixed, wo_ref[...], preferred_element_type=_F32)


def _merge(x, gain, ao, rf, rb, rg, ret_gain, w_gate, b_gate, w_ba, w_br, w_out):
    n, d = x.shape
    tm = TOKEN_TILE
    row = lambda w: pl.BlockSpec((tm, w), lambda i: (i, 0))
    vec = lambda w: pl.BlockSpec((1, w), lambda i: (0, 0))
    return pl.pallas_call(
        _merge_kernel,
        grid=(n // tm,),
        in_specs=[row(d), vec(d), row(ATTN_Q_W), row(RET_V_W), row(RET_V_W), row(RET_V_W),
                  vec(RET_V_W), _vmem_spec(), vec(2 * d), _vmem_spec(), _vmem_spec(), _vmem_spec()],
        out_specs=row(d),
        out_shape=jax.ShapeDtypeStruct((n, d), _F32),
        compiler_params=_params("parallel"),
        name="merge",
    )(x, gain, ao, rf, rb, rg, ret_gain, w_gate, b_gate, w_ba, w_br, w_out)


def _trunk(xs, p):
    depth = p["ffn1_w13"].shape[0]
    shapes = [x.shape[:2] for x in xs]
    xs = [x.reshape(-1, D_MODEL) for x in xs]
    cos, sin = _rope_tables(max(t for _, t in shapes))

    seg_id = jnp.arange(MXU_TILE) // HEAD_DIM
    seg = (seg_id[:, None] == seg_id[None, :]).astype(_BF16)
    bf = lambda w: w.astype(_BF16)
    vec = lambda g: g.astype(_F32)[None, :]
    final_gain = vec(p["final_norm"])

    for l in range(depth):
        ffn1 = (vec(p["ffn1_norm"][l]), bf(p["ffn1_w13"][l]), bf(p["ffn1_w2"][l]), final_gain)
        ffn2 = (vec(p["ffn2_norm"][l]), bf(p["ffn2_w13"][l]), bf(p["ffn2_w2"][l]), final_gain)
        mix_gain = vec(p["mix_norm"][l])
        proj = (mix_gain, bf(p["w_in"][l]), vec(jnp.tile(p["q_norm"][l], ATTN_HEADS)),
                vec(jnp.tile(p["k_norm"][l], ATTN_KV_HEADS)), seg, cos, sin)
        dec = jnp.concatenate([p["ret_decay_fwd"][l], p["ret_decay_bwd"][l]]).astype(_F32)
        dec = jnp.broadcast_to(dec[:, None], (2 * RET_HEADS, RET_CHUNK))
        merge = (vec(p["ret_norm"][l]), bf(p["w_gate"][l]), vec(p["b_gate"][l]),
                 bf(p["w_branch_attn"][l]), bf(p["w_branch_ret"][l]), bf(p["w_out"][l]))
        for i, (b, t) in enumerate(shapes):
            x = _ffn(xs[i], *ffn1, False)
            q, k, v, rq, rk, rv, rg = _inproj(x, *proj, t)
            ao = _attention(q, k, v, b, t)
            rf, rb = _retention(rq, rk, rv, dec, b, t)
            x = _merge(x, mix_gain, ao, rf, rb, rg, *merge)
            xs[i] = _ffn(x, *ffn2, l == depth - 1)

    return [x.reshape(b, t, D_MODEL) for x, (b, t) in zip(xs, shapes)]


def kernel(x_prompt, x_sample, ffn1_norm, ffn1_w13, ffn1_w2, mix_norm, w_in, q_norm, k_norm,
           ret_decay_fwd, ret_decay_bwd, ret_norm, w_branch_attn, w_branch_ret, w_gate, b_gate,
           w_out, ffn2_norm, ffn2_w13, ffn2_w2, final_norm):
    p = dict(ffn1_norm=ffn1_norm, ffn1_w13=ffn1_w13, ffn1_w2=ffn1_w2, mix_norm=mix_norm, w_in=w_in,
             q_norm=q_norm, k_norm=k_norm, ret_decay_fwd=ret_decay_fwd, ret_decay_bwd=ret_decay_bwd,
             ret_norm=ret_norm, w_branch_attn=w_branch_attn, w_branch_ret=w_branch_ret, w_gate=w_gate,
             b_gate=b_gate, w_out=w_out, ffn2_norm=ffn2_norm, ffn2_w13=ffn2_w13, ffn2_w2=ffn2_w2,
             final_norm=final_norm)
    y_prompt, y_sample = _trunk([x_prompt, x_sample], p)
    return (y_prompt, y_sample)
```
